```python
import math
import jax, jax.numpy as jnp
from jax import lax
import numpy as np

D_MODEL = 1024
BATCH = 4
SEQ = 8192
DEPTH = 4

GRID_W = 64
CTX_LEN = 256
N_MIXERS = 3
N_HEADS = 8
HEAD_DIM = D_MODEL // (2 * N_HEADS)
V_DIM = 2 * HEAD_DIM
Q_BLOCK = 128
ROPE_BASE = 10000.0
CONV_WIDTH = 3
GROUP_SIZE = 16
N_GROUPS = D_MODEL // GROUP_SIZE
STATE_DIM = 64
DT_MIN = 1e-3
DT_MAX = 1e-1
D_FF = (D_MODEL * 11) // 4
N_EXPERTS = 8
TOP_K = 2
EPS = 1e-6

kernel_name = 'hybrid_conv_diffattn_s5_moe_prefix_dit'


def rmsnorm(x, g):
    xf = x.astype(jnp.float32)
    y = xf * lax.rsqrt(jnp.mean(xf * xf, axis=-1, keepdims=True) + EPS)
    return (y * g.astype(jnp.float32)).astype(x.dtype)


def adaln(cond, w, b):
    m = (jax.nn.silu(cond) @ w + b).reshape(cond.shape[0], 1, 6, D_MODEL)
    return [m[:, :, j] for j in range(6)]


def modulate(h, g, shift, scale):
    return rmsnorm(h, g) * (1.0 + scale) + shift


def axial_rope_tables(n):
    rows = n // GRID_W
    row = jnp.repeat(jnp.arange(rows, dtype=jnp.float32), GRID_W)
    col = jnp.tile(jnp.arange(GRID_W, dtype=jnp.float32), rows)
    n_freq = HEAD_DIM // 4
    inv = ROPE_BASE ** (-jnp.arange(n_freq, dtype=jnp.float32) / n_freq)
    ang = jnp.concatenate([row[:, None] * inv, col[:, None] * inv], axis=-1)
    return jnp.cos(ang), jnp.sin(ang)


def _rotate(x, cos, sin):
    x1, x2 = jnp.split(x, 2, axis=-1)
    return jnp.concatenate([x1 * cos - x2 * sin, x2 * cos + x1 * sin], axis=-1)


def axial_rope(x, cos, sin):
    f = HEAD_DIM // 4
    c = cos[:, None, None, :]
    s = sin[:, None, None, :]
    xr, xc = jnp.split(x.astype(jnp.float32), 2, axis=-1)
    out = jnp.concatenate([_rotate(xr, c[..., :f], s[..., :f]), _rotate(xc, c[..., f:], s[..., f:])], axis=-1)
    return out.astype(x.dtype)


def short_conv_mixer(h, w_in, w_conv, w_out):
    b_gate, c_gate, v = jnp.split(h @ w_in, 3, axis=-1)
    z = c_gate * v
    L = z.shape[1]
    zp = jnp.pad(z, ((0, 0), (1, 1), (0, 0)))
    y = w_conv[0] * zp[:, :L] + w_conv[1] * zp[:, 1:L + 1] + w_conv[2] * zp[:, 2:]
    return (b_gate * y) @ w_out


def diff_softmax_attend(q, k, v, lam, lam_init, subln):
    s = jnp.einsum('bqhmd,bkhmd->bhmqk', q, k).astype(jnp.float32) * (HEAD_DIM ** -0.5)
    p = jax.nn.softmax(s, axis=-1)
    a = p[:, :, 0] - lam * p[:, :, 1]
    o = jnp.einsum('bhqk,bkhe->bqhe', a, v.astype(jnp.float32))
    o = rmsnorm(o, subln) * (1.0 - lam_init)
    return o.reshape(o.shape[0], o.shape[1], N_HEADS * V_DIM)


def diff_attention_mixer(hc, hl, w_qkv, lam_vecs, subln, w_o, cos, sin, lam_init, need_ctx):
    B, L, _ = hl.shape
    Lc = hc.shape[1]
    lv = lam_vecs.astype(jnp.float32)
    lam = jnp.exp(jnp.sum(lv[0] * lv[1])) - jnp.exp(jnp.sum(lv[2] * lv[3])) + lam_init
    q_l, k_l, v_l = jnp.split(hl @ w_qkv, 3, axis=-1)
    q_l = axial_rope(q_l.reshape(B, L, N_HEADS, 2, HEAD_DIM), cos, sin)
    k_l = axial_rope(k_l.reshape(B, L, N_HEADS, 2, HEAD_DIM), cos, sin)
    v_l = v_l.reshape(B, L, N_HEADS, V_DIM)
    k_c, v_c = jnp.split(hc @ w_qkv[:, D_MODEL:], 2, axis=-1)
    k_c = k_c.reshape(B, Lc, N_HEADS, 2, HEAD_DIM)
    v_c = v_c.reshape(B, Lc, N_HEADS, V_DIM)
    k_all = jnp.concatenate([k_c, k_l], axis=1)
    v_all = jnp.concatenate([v_c, v_l], axis=1)
    n_blocks = L // Q_BLOCK
    q_blocks = q_l.reshape(B, n_blocks, Q_BLOCK, N_HEADS, 2, HEAD_DIM).swapaxes(0, 1)
    o_blocks = lax.map(lambda qb: diff_softmax_attend(qb, k_all, v_all, lam, lam_init, subln), q_blocks)
    y_l = o_blocks.swapaxes(0, 1).reshape(B, L, N_HEADS * V_DIM).astype(hl.dtype) @ w_o
    y_c = None
    if need_ctx:
        q_c = (hc @ w_qkv[:, :D_MODEL]).reshape(B, Lc, N_HEADS, 2, HEAD_DIM)
        y_c = diff_softmax_attend(q_c, k_c, v_c, lam, lam_init, subln).astype(hc.dtype) @ w_o
    return y_c, y_l


def s5_discretise(a_re, a_im, log_dt, b_re, b_im):
    f32 = jnp.float32
    a_re, a_im = a_re.astype(f32), a_im.astype(f32)
    b_re, b_im = b_re.astype(f32), b_im.astype(f32)
    dt = jnp.exp(log_dt.astype(f32))[:, None]
    mag = jnp.exp(dt * a_re)
    ab_re = mag * jnp.cos(dt * a_im)
    ab_im = mag * jnp.sin(dt * a_im)
    den = a_re * a_re + a_im * a_im
    nr = ab_re - 1.0
    co_re = (nr * a_re + ab_im * a_im) / den
    co_im = (ab_im * a_re - nr * a_im) / den
    bb_re = co_re[..., None] * b_re - co_im[..., None] * b_im
    bb_im = co_re[..., None] * b_im + co_im[..., None] * b_re
    return ab_re, ab_im, bb_re, bb_im


def _linear_recurrence_combine(e1, e2):
    a1r, a1i, b1r, b1i = e1
    a2r, a2i, b2r, b2i = e2
    return (a2r * a1r - a2i * a1i, a2r * a1i + a2i * a1r,
            a2r * b1r - a2i * b1i + b2r, a2r * b1i + a2i * b1r + b2i)


def s5_scan(u, disc, h0):
    ab_re, ab_im, bb_re, bb_im = disc
    bu_re = jnp.einsum('blgs,gps->blgp', u, bb_re)
    bu_im = jnp.einsum('blgs,gps->blgp', u, bb_im)
    if h0 is not None:
        h_re, h_im = h0
        bu_re = bu_re.at[:, 0].add(ab_re * h_re - ab_im * h_im)
        bu_im = bu_im.at[:, 0].add(ab_re * h_im + ab_im * h_re)
    L = u.shape[1]
    a_re = jnp.broadcast_to(ab_re, (L,) + ab_re.shape)
    a_im = jnp.broadcast_to(ab_im, (L,) + ab_im.shape)
    def scan_one(br, bi):
        out = lax.associative_scan(_linear_recurrence_combine, (a_re, a_im, br, bi), axis=0)
        return out[2], out[3]
    return jax.vmap(scan_one)(bu_re, bu_im)


def s5_readout(h_re, h_im, c_re, c_im):
    return jnp.einsum('blgp,gsp->blgs', h_re, c_re) - jnp.einsum('blgp,gsp->blgs', h_im, c_im)


def _flip(t, rev):
    return t[:, ::-1] if rev else t


def s5_mixer(hc, hl, a_re, a_im, log_dt, b_re, b_im, c_re, c_im, d_skip, w_glu, need_ctx):
    f32 = jnp.float32
    B, L, _ = hl.shape
    Lc = hc.shape[1]
    u_c = hc.astype(f32).reshape(B, Lc, N_GROUPS, GROUP_SIZE)
    u_l = hl.astype(f32).reshape(B, L, N_GROUPS, GROUP_SIZE)
    d = d_skip.astype(f32)
    y_l = d * hl.astype(f32)
    y_c = d * hc.astype(f32) if need_ctx else None
    for direction in range(2):
        rev = direction == 1
        disc = s5_discretise(a_re[direction], a_im[direction], log_dt[direction], b_re[direction], b_im[direction])
        cr = c_re[direction].astype(f32)
        ci = c_im[direction].astype(f32)
        sc_re, sc_im = s5_scan(_flip(u_c, rev), disc, None)
        sl_re, sl_im = s5_scan(_flip(u_l, rev), disc, (sc_re[:, -1], sc_im[:, -1]))
        y_l = y_l + _flip(s5_readout(sl_re, sl_im, cr, ci), rev).reshape(B, L, D_MODEL)
        if need_ctx:
            y_c = y_c + _flip(s5_readout(sc_re, sc_im, cr, ci), rev).reshape(B, Lc, D_MODEL)
    def glu(y, dtype):
        val, gate = jnp.split(jax.nn.gelu(y).astype(dtype) @ w_glu, 2, axis=-1)
        return val * jax.nn.sigmoid(gate)
    y_c_out = glu(y_c, hc.dtype) if need_ctx else None
    return y_c_out, glu(y_l, hl.dtype)


def swiglu(h, w_gu, w_down):
    g, u = jnp.split(h @ w_gu, 2, axis=-1)
    return (jax.nn.silu(g) * u) @ w_down


def moe_swiglu(h, w_router, w_gu, w_down):
    f32 = jnp.float32
    logits = (h @ w_router).astype(f32)
    top_v, top_i = lax.top_k(logits, TOP_K)
    top_w = jax.nn.softmax(top_v, axis=-1)
    gates = jnp.sum(jax.nn.one_hot(top_i, N_EXPERTS, dtype=f32) * top_w[..., None], axis=-2)
    out = jnp.zeros(h.shape, f32)
    for e in range(N_EXPERTS):
        out = out + gates[..., e:e + 1] * swiglu(h, w_gu[e], w_down[e])
    return out.astype(h.dtype)


def setup_inputs(seed: int = 0) -> dict:
    key = jax.random.key(seed)
    keys = iter(jax.random.split(key, 16 * DEPTH + 8))
    f32 = jnp.float32
    D = D_MODEL
    def nrm(shape, scale=1.0):
        return jax.random.normal(next(keys), shape, f32) * scale
    def gain(n):
        return 1.0 + nrm((n,), 0.02)
    inp = {'x': nrm((BATCH, SEQ, D)), 'c': nrm((BATCH, D)),
           'ctx': nrm((BATCH, CTX_LEN, D)), 'c_ctx': nrm((D,))}
    for i in range(DEPTH):
        p = 'l%d_' % i
        inp[p + 'ada_w'] = nrm((D, 6 * D), 0.5 * D ** -0.5)
        inp[p + 'ada_b'] = nrm((6 * D,), 0.02)
        inp[p + 'norm_mix'] = gain(D)
        inp[p + 'norm_ffn'] = gain(D)
        kind = i % N_MIXERS
        if kind == 0:
            inp[p + 'conv_w_in'] = nrm((D, 3 * D), D ** -0.5)
            inp[p + 'conv_w'] = nrm((CONV_WIDTH, D), CONV_WIDTH ** -0.5)
            inp[p + 'conv_w_out'] = nrm((D, D), D ** -0.5)
        elif kind == 1:
            inp[p + 'attn_w_qkv'] = nrm((D, 3 * D), D ** -0.5)
            inp[p + 'attn_lam'] = nrm((4, HEAD_DIM), 0.1)
            inp[p + 'attn_subln'] = gain(V_DIM)
            inp[p + 'attn_w_o'] = nrm((D, D), D ** -0.5)
        else:
            inp[p + 'ssm_a_re'] = -0.5 + nrm((2, N_GROUPS, STATE_DIM), 0.01)
            inp[p + 'ssm_a_im'] = math.pi * jnp.arange(STATE_DIM, dtype=f32) + nrm((2, N_GROUPS, STATE_DIM), 0.01)
            inp[p + 'ssm_log_dt'] = jax.random.uniform(next(keys), (2, N_GROUPS), f32, math.log(DT_MIN), math.log(DT_MAX))
            inp[p + 'ssm_b_re'] = nrm((2, N_GROUPS, STATE_DIM, GROUP_SIZE), (2 * GROUP_SIZE) ** -0.5)
            inp[p + 'ssm_b_im'] = nrm((2, N_GROUPS, STATE_DIM, GROUP_SIZE), (2 * GROUP_SIZE) ** -0.5)
            inp[p + 'ssm_c_re'] = nrm((2, N_GROUPS, GROUP_SIZE, STATE_DIM), STATE_DIM ** -0.5)
            inp[p + 'ssm_c_im'] = nrm((2, N_GROUPS, GROUP_SIZE, STATE_DIM), STATE_DIM ** -0.5)
            inp[p + 'ssm_d'] = nrm((D,))
            inp[p + 'ssm_w_glu'] = nrm((D, 2 * D), D ** -0.5)
        if i % 2 == 0:
            inp[p + 'ffn_w_gu'] = nrm((D, 2 * D_FF), D ** -0.5)
            inp[p + 'ffn_w_down'] = nrm((D_FF, D), D_FF ** -0.5)
        else:
            inp[p + 'moe_router'] = nrm((D, N_EXPERTS), D ** -0.5)
            inp[p + 'moe_w_gu'] = nrm((N_EXPERTS, D, 2 * D_FF), D ** -0.5)
            inp[p + 'moe_w_down'] = nrm((N_EXPERTS, D_FF, D), D_FF ** -0.5)
    inp['final_norm'] = gain(D)
    return inp


def reference(x, c, ctx, c_ctx,
              l0_ada_w, l0_ada_b, l0_norm_mix, l0_norm_ffn, l0_conv_w_in, l0_conv_w, l0_conv_w_out, l0_ffn_w_gu, l0_ffn_w_down,
              l1_ada_w, l1_ada_b, l1_norm_mix, l1_norm_ffn, l1_attn_w_qkv, l1_attn_lam, l1_attn_subln, l1_attn_w_o, l1_moe_router, l1_moe_w_gu, l1_moe_w_down,
              l2_ada_w, l2_ada_b, l2_norm_mix, l2_norm_ffn, l2_ssm_a_re, l2_ssm_a_im, l2_ssm_log_dt, l2_ssm_b_re, l2_ssm_b_im, l2_ssm_c_re, l2_ssm_c_im, l2_ssm_d, l2_ssm_w_glu, l2_ffn_w_gu, l2_ffn_w_down,
              l3_ada_w, l3_ada_b, l3_norm_mix, l3_norm_ffn, l3_conv_w_in, l3_conv_w, l3_conv_w_out, l3_moe_router, l3_moe_w_gu, l3_moe_w_down,
              final_norm):
    layers = [
        dict(ada=(l0_ada_w, l0_ada_b), norms=(l0_norm_mix, l0_norm_ffn),
             mixer=(l0_conv_w_in, l0_conv_w, l0_conv_w_out), ffn=(l0_ffn_w_gu, l0_ffn_w_down)),
        dict(ada=(l1_ada_w, l1_ada_b), norms=(l1_norm_mix, l1_norm_ffn),
             mixer=(l1_attn_w_qkv, l1_attn_lam, l1_attn_subln, l1_attn_w_o), ffn=(l1_moe_router, l1_moe_w_gu, l1_moe_w_down)),
        dict(ada=(l2_ada_w, l2_ada_b), norms=(l2_norm_mix, l2_norm_ffn),
             mixer=(l2_ssm_a_re, l2_ssm_a_im, l2_ssm_log_dt, l2_ssm_b_re, l2_ssm_b_im, l2_ssm_c_re, l2_ssm_c_im, l2_ssm_d, l2_ssm_w_glu),
             ffn=(l2_ffn_w_gu, l2_ffn_w_down)),
        dict(ada=(l3_ada_w, l3_ada_b), norms=(l3_norm_mix, l3_norm_ffn),
             mixer=(l3_conv_w_in, l3_conv_w, l3_conv_w_out), ffn=(l3_moe_router, l3_moe_w_gu, l3_moe_w_down)),
    ]
    n = x.shape[1]
    n_ctx = ctx.shape[1]
    cos, sin = axial_rope_tables(n)
    kinds = [i % N_MIXERS for i in range(DEPTH)]
    h_ctx = ctx
    for i in range(DEPTH):
        p = layers[i]
        kind = kinds[i]
        reads_ctx = kind != 0
        ctx_next = any(k != 0 for k in kinds[i + 1:])
        use_ctx = reads_ctx or ctx_next
        norm_mix, norm_ffn = p['norms']
        sh1, sc1, g1, sh2, sc2, g2 = adaln(c, *p['ada'])
        hl = modulate(x, norm_mix, sh1, sc1)
        hc = None
        if use_ctx:
            csh1, csc1, cg1, csh2, csc2, cg2 = adaln(c_ctx[None], *p['ada'])
            hc = modulate(h_ctx, norm_mix, csh1, csc1)
        if kind == 0:
            y_l = short_conv_mixer(hl, *p['mixer'])
            y_c = short_conv_mixer(hc, *p['mixer']) if ctx_next else None
        elif kind == 1:
            y_c, y_l = diff_attention_mixer(hc, hl, *p['mixer'], cos=cos, sin=sin,
                                            lam_init=0.8 - 0.6 * math.exp(-0.3 * i), need_ctx=ctx_next)
        else:
            y_c, y_l = s5_mixer(hc, hl, *p['mixer'], need_ctx=ctx_next)
        x = x + g1 * y_l
        if ctx_next:
            h_ctx = h_ctx + cg1 * y_c
        ffn = swiglu if i % 2 == 0 else moe_swiglu
        hl2 = modulate(x, norm_ffn, sh2, sc2)
        if ctx_next:
            hc2 = modulate(h_ctx, norm_ffn, csh2, csc2)
            out = ffn(jnp.concatenate([hc2, hl2], axis=1), *p['ffn'])
            h_ctx = h_ctx + cg2 * out[:, :n_ctx]
            x = x + g2 * out[:, n_ctx:]
        else:
            x = x + g2 * ffn(hl2, *p['ffn'])
    return rmsnorm(x, final_norm)
```

```python
import functools
import math

import jax
import jax.numpy as jnp
from jax import lax
from jax.experimental import pallas as pl
from jax.experimental.pallas import tpu as pltpu

F32 = jnp.float32
BF16 = jnp.bfloat16
HIGHEST = lax.Precision.HIGHEST

D = 1024
GRID_W = 64
N_HEADS = 8
HEAD_DIM = 64
V_DIM = 128
ROPE_BASE = 10000.0
GROUP_SIZE = 16
N_GROUPS = 64
STATE_DIM = 64
D_FF = 2816
N_EXPERTS = 8
EPS = 1e-6

TM = 256
FF_CHUNK = 256
TK = 256
S5_T = 16
TME = 256
MIB = 1 << 20


def _sds(shape, dtype):
    return jax.ShapeDtypeStruct(shape, dtype)


def _params(sem, vmem_mib):
    return pltpu.CompilerParams(dimension_semantics=sem, vmem_limit_bytes=vmem_mib * MIB)


def _modulate(x, g, shift, scale):
    y = x * lax.rsqrt(jnp.mean(x * x, axis=-1, keepdims=True) + EPS)
    return (y * g) * (1.0 + scale) + shift


def _x_spec():
    return pl.BlockSpec((None, TM, D), lambda b, i: (b, i, 0))


def _mod_spec(n_ctx_tiles):
    return pl.BlockSpec((None, None, 6, D), lambda b, i: (b, jnp.where(i >= n_ctx_tiles, 1, 0), 0, 0))


def _const_spec(shape):
    nd = len(shape)
    return pl.BlockSpec(shape, lambda b, i: (0,) * nd)


def _ada_kernel(c_ref, w_ref, b_ref, o_ref):
    c = c_ref[...]
    s = c * jax.nn.sigmoid(c)
    o_ref[...] = jnp.dot(s, w_ref[...], preferred_element_type=F32, precision=HIGHEST) + b_ref[...]


def _ada(cond8, w, b):
    nb = 4
    bn = w.shape[1] // nb
    return pl.pallas_call(
        _ada_kernel,
        out_shape=_sds((8, w.shape[1]), F32),
        grid=(nb,),
        in_specs=[pl.BlockSpec((8, D), lambda j: (0, 0)),
                  pl.BlockSpec((D, bn), lambda j: (0, j)),
                  pl.BlockSpec((1, bn), lambda j: (0, j))],
        out_specs=pl.BlockSpec((8, bn), lambda j: (0, j)),
        compiler_params=_params(("parallel",), 40),
        name="ada",
    )(cond8, w, b.reshape(1, -1))


def _conv_in_kernel(x_ref, mod_ref, g_ref, w_ref, b_out, z_out):
    h = _modulate(x_ref[...], g_ref[...], mod_ref[0:1, :], mod_ref[1:2, :]).astype(BF16)
    b_out[...] = jnp.dot(h, w_ref[:, 0:D], preferred_element_type=F32).astype(BF16)
    c_gate = jnp.dot(h, w_ref[:, D:2 * D], preferred_element_type=F32)
    v = jnp.dot(h, w_ref[:, 2 * D:3 * D], preferred_element_type=F32)
    z_out[...] = c_gate * v


def _conv_out_kernel(z_ref, zp_ref, zn_ref, b_ref, cw_ref, w_ref, x_ref, mod_ref, o_ref, *, n_ctx, n_tot):
    i = pl.program_id(1)
    z = z_ref[...]
    row = lax.broadcasted_iota(jnp.int32, (TM, 1), 0)
    grow = row + i * TM
    prev_row = zp_ref[7:8, :]
    next_row = zn_ref[0:1, :]
    zm1 = jnp.where(row == 0, prev_row, pltpu.roll(z, 1, axis=0))
    zp1 = jnp.where(row == TM - 1, next_row, pltpu.roll(z, TM - 1, axis=0))
    zm1 = jnp.where((grow == 0) | (grow == n_ctx), 0.0, zm1)
    zp1 = jnp.where((grow == n_ctx - 1) | (grow == n_tot - 1), 0.0, zp1)
    y = cw_ref[0:1, :] * zm1 + cw_ref[1:2, :] * z + cw_ref[2:3, :] * zp1
    a = (b_ref[...].astype(F32) * y).astype(BF16)
    o_ref[...] = x_ref[...] + mod_ref[2:3, :] * jnp.dot(a, w_ref[...], preferred_element_type=F32)


def _conv_mixer(x, mods, norm_g, w_in, conv_w, w_out, n_ctx):
    B, T, _ = x.shape
    nt = T // TM
    nct = n_ctx // TM
    b_gate, z = pl.pallas_call(
        _conv_in_kernel,
        out_shape=(_sds((B, T, D), BF16), _sds((B, T, D), F32)),
        grid=(B, nt),
        in_specs=[_x_spec(), _mod_spec(nct), _const_spec((1, D)), _const_spec((D, 3 * D))],
        out_specs=(_x_spec(), _x_spec()),
        compiler_params=_params(("parallel", "parallel"), 48),
        name="conv_in",
    )(x, mods, norm_g.reshape(1, D), w_in.astype(BF16))
    r8 = TM // 8
    return pl.pallas_call(
        functools.partial(_conv_out_kernel, n_ctx=n_ctx, n_tot=T),
        out_shape=_sds((B, T, D), F32),
        grid=(B, nt),
        in_specs=[_x_spec(),
                  pl.BlockSpec((None, 8, D), lambda b, i: (b, jnp.maximum(i * r8 - 1, 0), 0)),
                  pl.BlockSpec((None, 8, D), lambda b, i: (b, jnp.minimum((i + 1) * r8, T // 8 - 1), 0)),
                  _x_spec(), _const_spec((3, D)), _const_spec((D, D)), _x_spec(), _mod_spec(nct)],
        out_specs=_x_spec(),
        compiler_params=_params(("parallel", "parallel"), 48),
        name="conv_out",
    )(z, z, z, b_gate, conv_w, w_out.astype(BF16), x, mods)


def _qkv_kernel(x_ref, mod_ref, g_ref, w_ref, cos_ref, slo_ref, shi_ref, q_out, k_out, v_out):
    h = _modulate(x_ref[...], g_ref[...], mod_ref[0:1, :], mod_ref[1:2, :]).astype(BF16)
    cos = cos_ref[...]
    s_lo = slo_ref[...]
    s_hi = shi_ref[...]

    def rope(t):
        return t * cos + pltpu.roll(t, 128 - 16, axis=1) * s_lo + pltpu.roll(t, 16, axis=1) * s_hi

    for hd in range(N_HEADS):
        lo, hi = hd * 128, (hd + 1) * 128
        q = jnp.dot(h, w_ref[:, lo:hi], preferred_element_type=F32) * (HEAD_DIM ** -0.5)
        q_out[:, lo:hi] = rope(q).astype(BF16)
        k = jnp.dot(h, w_ref[:, D + lo:D + hi], preferred_element_type=F32)
        k_out[:, lo:hi] = rope(k).astype(BF16)
    v_out[...] = jnp.dot(h, w_ref[:, 2 * D:3 * D], preferred_element_type=F32).astype(BF16)


def _attn_kernel(q_ref, k_ref, v_ref, lam_ref, sub_ref, o_ref, *, n_ctx_tiles, n_key_chunks, lam_init):
    i = pl.program_id(2)
    q = q_ref[...]
    lane = lax.broadcasted_iota(jnp.int32, (TM, 128), 1)
    q1 = jnp.where(lane < HEAD_DIM, q, jnp.zeros_like(q))
    q2 = jnp.where(lane >= HEAD_DIM, q, jnp.zeros_like(q))
    nt_dims = (((1,), (1,)), ((), ()))

    def body(j, carry):
        m1, l1, a1, m2, l2, a2 = carry
        start = pl.multiple_of(j * TK, TK)
        kc = k_ref[pl.ds(start, TK), :]
        vc = v_ref[pl.ds(start, TK), :]

        def one(qm, m, l, a):
            s = lax.dot_general(qm, kc, nt_dims, preferred_element_type=F32)
            m_new = jnp.maximum(m, jnp.max(s, axis=-1, keepdims=True))
            p = jnp.exp(s - m_new)
            alpha = jnp.exp(m - m_new)
            l_new = alpha * l + jnp.sum(p, axis=-1, keepdims=True)
            a_new = alpha * a + jnp.dot(p.astype(BF16), vc, preferred_element_type=F32)
            return m_new, l_new, a_new

        m1, l1, a1 = one(q1, m1, l1, a1)
        m2, l2, a2 = one(q2, m2, l2, a2)
        return m1, l1, a1, m2, l2, a2

    neg = jnp.full((TM, 1), -1e30, F32)
    zero1 = jnp.zeros((TM, 1), F32)
    zacc = jnp.zeros((TM, V_DIM), F32)
    n_chunks = jnp.where(i < n_ctx_tiles, n_ctx_tiles * (TM // TK), n_key_chunks)
    m1, l1, a1, m2, l2, a2 = lax.fori_loop(0, n_chunks, body, (neg, zero1, zacc, neg, zero1, zacc))

    lv = lam_ref[...]
    lam = (jnp.exp(jnp.sum(lv[0:1, :] * lv[1:2, :], axis=-1, keepdims=True))
           - jnp.exp(jnp.sum(lv[2:3, :] * lv[3:4, :], axis=-1, keepdims=True)) + lam_init)
    o = a1 / l1 - lam * (a2 / l2)
    o = o * lax.rsqrt(jnp.mean(o * o, axis=-1, keepdims=True) + EPS)
    o_ref[...] = ((o * sub_ref[...]) * (1.0 - lam_init)).astype(BF16)


def _out_proj_kernel(a_ref, w_ref, x_ref, mod_ref, o_ref):
    o_ref[...] = x_ref[...] + mod_ref[2:3, :] * jnp.dot(a_ref[...], w_ref[...], preferred_element_type=F32)


def _rope_tables(n_ctx, n_lat):
    t = jnp.arange(n_lat, dtype=jnp.int32)
    row = (t // GRID_W).astype(F32)
    col = (t % GRID_W).astype(F32)
    n_freq = HEAD_DIM // 4
    inv = ROPE_BASE ** (-jnp.arange(n_freq, dtype=F32) / n_freq)
    ang_r = row[:, None] * inv
    ang_c = col[:, None] * inv
    zero = jnp.zeros_like(ang_r)
    cos64 = jnp.concatenate([jnp.cos(ang_r), jnp.cos(ang_r), jnp.cos(ang_c), jnp.cos(ang_c)], axis=-1)
    slo64 = jnp.concatenate([-jnp.sin(ang_r), zero, -jnp.sin(ang_c), zero], axis=-1)
    shi64 = jnp.concatenate([zero, jnp.sin(ang_r), zero, jnp.sin(ang_c)], axis=-1)

    def full(tab, ctx_val):
        tab = jnp.concatenate([tab, tab], axis=-1)
        return jnp.concatenate([jnp.full((n_ctx, 128), ctx_val, F32), tab], axis=0)

    return full(cos64, 1.0), full(slo64, 0.0), full(shi64, 0.0)


def _attn_mixer(x, mods, norm_g, w_qkv, lam_vecs, subln, w_o, n_ctx, lam_init):
    B, T, _ = x.shape
    nt = T // TM
    nct = n_ctx // TM
    cos, s_lo, s_hi = _rope_tables(n_ctx, T - n_ctx)
    tab_spec = pl.BlockSpec((TM, 128), lambda b, i: (i, 0))
    q, k, v = pl.pallas_call(
        _qkv_kernel,
        out_shape=(_sds((B, T, D), BF16),) * 3,
        grid=(B, nt),
        in_specs=[_x_spec(), _mod_spec(nct), _const_spec((1, D)), _const_spec((D, 3 * D)),
                  tab_spec, tab_spec, tab_spec],
        out_specs=(_x_spec(),) * 3,
        compiler_params=_params(("parallel", "parallel"), 48),
        name="qkv",
    )(x, mods, norm_g.reshape(1, D), w_qkv.astype(BF16), cos, s_lo, s_hi)

    head_q = pl.BlockSpec((None, TM, 128), lambda b, h, i: (b, i, h))
    head_kv = pl.BlockSpec((None, T, 128), lambda b, h, i: (b, 0, h))
    attn = pl.pallas_call(
        functools.partial(_attn_kernel, n_ctx_tiles=nct, n_key_chunks=T // TK, lam_init=lam_init),
        out_shape=_sds((B, T, D), BF16),
        grid=(B, N_HEADS, nt),
        in_specs=[head_q, head_kv, head_kv,
                  pl.BlockSpec((4, HEAD_DIM), lambda b, h, i: (0, 0)),
                  pl.BlockSpec((1, V_DIM), lambda b, h, i: (0, 0))],
        out_specs=head_q,
        compiler_params=_params(("parallel", "parallel", "parallel"), 48),
        name="diff_attn",
    )(q, k, v, lam_vecs, subln.reshape(1, V_DIM))

    return pl.pallas_call(
        _out_proj_kernel,
        out_shape=_sds((B, T, D), F32),
        grid=(B, nt),
        in_specs=[_x_spec(), _const_spec((D, D)), _x_spec(), _mod_spec(nct)],
        out_specs=_x_spec(),
        compiler_params=_params(("parallel", "parallel"), 48),
        name="attn_out",
    )(attn, w_o.astype(BF16), x, mods)


def _mod_only_kernel(x_ref, mod_ref, g_ref, h_out):
    h_out[...] = _modulate(x_ref[...], g_ref[...], mod_ref[0:1, :], mod_ref[1:2, :]).astype(BF16)


def _s5_state_kernel(u_ref, w_ref, sfr, sfi, sbr, sbi):
    s = jnp.dot(u_ref[...], w_ref[...], preferred_element_type=F32)
    sfr[...] = s[:, 0:128]
    sfi[...] = s[:, 128:256]
    sbr[...] = s[:, 256:384]
    sbi[...] = s[:, 384:512]


def _s5_scan_kernel(sfr, sfi, sbr, sbi, afr, afi, abr, abi, hfr, hfi, hbr, hbi, *, n_ctx_chunks, n_chunks):
    ar, ai = afr[...], afi[...]

    def fwd(n, carry):
        hr, hi = carry
        hfr[n] = hr
        hfi[n] = hi
        sr, si = sfr[n], sfi[n]
        return ar * hr - ai * hi + sr, ar * hi + ai * hr + si

    zero = jnp.zeros(ar.shape, F32)
    lax.fori_loop(0, n_chunks, fwd, (zero, zero))

    br, bi = abr[...], abi[...]

    def bwd(t, carry, top):
        n = top - t
        hr, hi = carry
        hbr[n] = hr
        hbi[n] = hi
        sr, si = sbr[n], sbi[n]
        return br * hr - bi * hi + sr, br * hi + bi * hr + si

    carry = lax.fori_loop(0, n_ctx_chunks, functools.partial(bwd, top=n_ctx_chunks - 1), (zero, zero))
    lax.fori_loop(0, n_chunks - n_ctx_chunks, functools.partial(bwd, top=n_chunks - 1), carry)


def _s5_out_kernel(u_ref, m_ref, hfr, hfi, hbr, hbi, v_ref, y_ref):
    u = u_ref[...]
    y0 = jnp.dot(u[:, 0:256], m_ref[0], preferred_element_type=F32)
    y1 = jnp.dot(u[:, 256:512], m_ref[1], preferred_element_type=F32)
    h = jnp.concatenate([hfr[...], hfi[...], hbr[...], hbi[...]], axis=-1).astype(BF16)
    y_ref[...] = jnp.concatenate([y0, y1], axis=-1) + jnp.dot(h, v_ref[...], preferred_element_type=F32)


def _glu_kernel(y_ref, x_ref, mod_ref, g_ref, d_ref, w_ref, o_ref):
    x = x_ref[...]
    hl = _modulate(x, g_ref[...], mod_ref[0:1, :], mod_ref[1:2, :])
    y = y_ref[...] + d_ref[...] * hl
    a = jax.nn.gelu(y).astype(BF16)
    val = jnp.dot(a, w_ref[:, 0:D], preferred_element_type=F32)
    gate = jnp.dot(a, w_ref[:, D:2 * D], preferred_element_type=F32)
    o_ref[...] = x + mod_ref[2:3, :] * (val * jax.nn.sigmoid(gate))


def _s5_matrices(a_re, a_im, log_dt, b_re, b_im, c_re, c_im):
    T = S5_T
    hp = functools.partial(jnp.einsum, precision=HIGHEST)
    pw_re, pw_im, bb_re, bb_im = [], [], [], []
    for d in range(2):
        ar, ai = a_re[d].astype(F32), a_im[d].astype(F32)
        dt = jnp.exp(log_dt[d].astype(F32))[:, None]
        mag = jnp.exp(dt * ar)
        ab_re = mag * jnp.cos(dt * ai)
        ab_im = mag * jnp.sin(dt * ai)
        den = ar * ar + ai * ai
        nr = ab_re - 1.0
        co_re = (nr * ar + ab_im * ai) / den
        co_im = (ab_im * ar - nr * ai) / den
        br, bi = b_re[d].astype(F32), b_im[d].astype(F32)
        bb_re.append(co_re[..., None] * br - co_im[..., None] * bi)
        bb_im.append(co_re[..., None] * bi + co_im[..., None] * br)
        pr = [jnp.ones_like(ab_re)]
        pi = [jnp.zeros_like(ab_re)]
        for _ in range(T):
            pr.append(pr[-1] * ab_re - pi[-1] * ab_im)
            pi.append(pr[-2] * ab_im + pi[-1] * ab_re)
        pw_re.append(jnp.stack(pr))
        pw_im.append(jnp.stack(pi))

    def lag_kernels(d):
        mr = pw_re[d][:T, :, :, None] * bb_re[d][None] - pw_im[d][:T, :, :, None] * bb_im[d][None]
        mi = pw_re[d][:T, :, :, None] * bb_im[d][None] + pw_im[d][:T, :, :, None] * bb_re[d][None]
        return hp('gop,kgpi->kgoi', c_re[d].astype(F32), mr) - hp('gop,kgpi->kgoi', c_im[d].astype(F32), mi)

    kf, kb = lag_kernels(0), lag_kernels(1)
    lag0 = (kf[0] + kb[0])[None]
    kfull = jnp.concatenate([kb[:0:-1], lag0, kf[1:]], axis=0)
    tt = jnp.arange(T)
    idx = tt[None, :] - tt[:, None] + (T - 1)
    mt = kfull[idx]
    mt = mt.transpose(2, 0, 4, 1, 3).reshape(N_GROUPS, T * GROUP_SIZE, T * GROUP_SIZE)

    def end_state(d, powers):
        pr, pi = pw_re[d][powers], pw_im[d][powers]
        wr = pr[..., None] * bb_re[d][None] - pi[..., None] * bb_im[d][None]
        wi = pr[..., None] * bb_im[d][None] + pi[..., None] * bb_re[d][None]
        f = lambda w: w.transpose(1, 0, 3, 2).reshape(N_GROUPS, T * GROUP_SIZE, STATE_DIM)
        return f(wr), f(wi)

    def read_out(d, powers):
        pr, pi = pw_re[d][powers], pw_im[d][powers]
        cr, ci = c_re[d].astype(F32), c_im[d].astype(F32)
        qr = cr[None] * pr[:, :, None, :] - ci[None] * pi[:, :, None, :]
        qi = cr[None] * pi[:, :, None, :] + ci[None] * pr[:, :, None, :]
        f = lambda q: q.transpose(1, 3, 0, 2).reshape(N_GROUPS, STATE_DIM, T * GROUP_SIZE)
        return f(qr), f(-qi)

    wfr, wfi = end_state(0, T - 1 - tt)
    wbr, wbi = end_state(1, tt)
    vfr, vfi = read_out(0, tt + 1)
    vbr, vbi = read_out(1, T - tt)

    def pair_cols(parts):
        w = jnp.stack(parts, axis=2).reshape(N_GROUPS // 2, 2, T * GROUP_SIZE, 4, STATE_DIM)
        out = jnp.zeros((N_GROUPS // 2, 2, T * GROUP_SIZE, 4, 2, STATE_DIM), F32)
        out = out.at[:, 0, :, :, 0].set(w[:, 0]).at[:, 1, :, :, 1].set(w[:, 1])
        return out.reshape(N_GROUPS // 2, 2 * T * GROUP_SIZE, 4 * 2 * STATE_DIM)

    def pair_rows(parts):
        v = jnp.stack(parts, axis=1).reshape(N_GROUPS // 2, 2, 4, STATE_DIM, T * GROUP_SIZE)
        out = jnp.zeros((N_GROUPS // 2, 4, 2, STATE_DIM, 2, T * GROUP_SIZE), F32)
        out = out.at[:, :, 0, :, 0].set(v[:, 0]).at[:, :, 1, :, 1].set(v[:, 1])
        return out.reshape(N_GROUPS // 2, 4 * 2 * STATE_DIM, 2 * T * GROUP_SIZE)

    w2 = pair_cols([wfr, wfi, wbr, wbi])
    v2 = pair_rows([vfr, vfi, vbr, vbi])
    a_pow = [t[T].reshape(N_GROUPS // 2, 2 * STATE_DIM) for t in (pw_re[0], pw_im[0], pw_re[1], pw_im[1])]
    return mt, w2, v2, a_pow


def _s5_mixer(x, mods, norm_g, ssm, n_ctx):
    a_re, a_im, log_dt, b_re, b_im, c_re, c_im, d_skip, w_glu = ssm
    B, T, _ = x.shape
    nt = T // TM
    nct = n_ctx // TM
    nch = T // S5_T
    n_pairs = N_GROUPS // 2
    pw = 2 * S5_T * GROUP_SIZE
    mt, w2, v2, a_pow = _s5_matrices(a_re, a_im, log_dt, b_re, b_im, c_re, c_im)

    h = pl.pallas_call(
        _mod_only_kernel,
        out_shape=_sds((B, T, D), BF16),
        grid=(B, nt),
        in_specs=[_x_spec(), _mod_spec(nct), _const_spec((1, D))],
        out_specs=_x_spec(),
        compiler_params=_params(("parallel", "parallel"), 32),
        name="s5_mod",
    )(x, mods, norm_g.reshape(1, D))
    u2 = h.reshape(B, nch, S5_T, n_pairs, 2, GROUP_SIZE).transpose(3, 0, 1, 4, 2, 5).reshape(n_pairs, B * nch, pw)

    rows = B * nch
    state_shape = _sds((rows, n_pairs * 128), F32)
    pair_u = pl.BlockSpec((None, rows, pw), lambda p: (p, 0, 0))
    pair_mat = pl.BlockSpec((None, pw, pw), lambda p: (p, 0, 0))
    pair_state = pl.BlockSpec((rows, 128), lambda p: (0, p))
    s_parts = pl.pallas_call(
        _s5_state_kernel,
        out_shape=(state_shape,) * 4,
        grid=(n_pairs,),
        in_specs=[pair_u, pair_mat],
        out_specs=(pair_state,) * 4,
        compiler_params=_params(("parallel",), 48),
        name="s5_state",
    )(u2, w2.astype(BF16))

    npb = 8
    view = lambda t: t.reshape(B, nch, n_pairs, 128)
    scan_blk = pl.BlockSpec((None, nch, npb, 128), lambda b, j: (b, 0, j, 0))
    coef_blk = pl.BlockSpec((npb, 128), lambda b, j: (j, 0))
    h_parts = pl.pallas_call(
        functools.partial(_s5_scan_kernel, n_ctx_chunks=n_ctx // S5_T, n_chunks=nch),
        out_shape=(_sds((B, nch, n_pairs, 128), F32),) * 4,
        grid=(B, n_pairs // npb),
        in_specs=[scan_blk] * 4 + [coef_blk] * 4,
        out_specs=(scan_blk,) * 4,
        compiler_params=_params(("parallel", "parallel"), 56),
        name="s5_scan",
    )(*[view(t) for t in s_parts], *a_pow)

    y2 = pl.pallas_call(
        _s5_out_kernel,
        out_shape=_sds((n_pairs, rows, pw), F32),
        grid=(n_pairs,),
        in_specs=[pair_u, pl.BlockSpec((2, pw // 2, pw // 2), lambda p: (p, 0, 0))] + [pair_state] * 4 + [pair_mat],
        out_specs=pair_u,
        compiler_params=_params(("parallel",), 56),
        name="s5_out",
    )(u2, mt.astype(BF16), *[t.reshape(rows, n_pairs * 128) for t in h_parts], v2.astype(BF16))
    y = y2.reshape(n_pairs, B, nch, 2, S5_T, GROUP_SIZE).transpose(1, 2, 4, 0, 3, 5).reshape(B, T, D)

    return pl.pallas_call(
        _glu_kernel,
        out_shape=_sds((B, T, D), F32),
        grid=(B, nt),
        in_specs=[_x_spec(), _x_spec(), _mod_spec(nct), _const_spec((1, D)), _const_spec((1, D)),
                  _const_spec((D, 2 * D))],
        out_specs=_x_spec(),
        compiler_params=_params(("parallel", "parallel"), 48),
        name="s5_glu",
    )(y, x, mods, norm_g.reshape(1, D), d_skip.reshape(1, D), w_glu.astype(BF16))


def _swiglu_rows(h, wgu_ref, wd_ref):
    acc = jnp.zeros((h.shape[0], D), F32)
    for c in range(D_FF // FF_CHUNK):
        lo, hi = c * FF_CHUNK, (c + 1) * FF_CHUNK
        g = jnp.dot(h, wgu_ref[:, lo:hi], preferred_element_type=F32)
        u = jnp.dot(h, wgu_ref[:, D_FF + lo:D_FF + hi], preferred_element_type=F32)
        a = ((g * jax.nn.sigmoid(g)) * u).astype(BF16)
        acc = acc + jnp.dot(a, wd_ref[lo:hi, :], preferred_element_type=F32)
    return acc


def _ffn_kernel(x_ref, mod_ref, g_ref, wgu_ref, wd_ref, o_ref):
    x = x_ref[...]
    h = _modulate(x, g_ref[...], mod_ref[3:4, :], mod_ref[4:5, :]).astype(BF16)
    o_ref[...] = x + mod_ref[5:6, :] * _swiglu_rows(h, wgu_ref, wd_ref)


def _dense_ffn(x, mods, norm_g, w_gu, w_down, n_ctx):
    B, T, _ = x.shape
    return pl.pallas_call(
        _ffn_kernel,
        out_shape=_sds((B, T, D), F32),
        grid=(B, T // TM),
        in_specs=[_x_spec(), _mod_spec(n_ctx // TM), _const_spec((1, D)),
                  _const_spec((D, 2 * D_FF)), _const_spec((D_FF, D))],
        out_specs=_x_spec(),
        compiler_params=_params(("parallel", "parallel"), 56),
        name="dense_ffn",
    )(x, mods, norm_g.reshape(1, D), w_gu.astype(BF16), w_down.astype(BF16))


def _router_kernel(x_ref, mod_ref, g_ref, wr_ref, h_out, r_out):
    h = _modulate(x_ref[...], g_ref[...], mod_ref[3:4, :], mod_ref[4:5, :])
    h_out[...] = h
    logits = jnp.dot(h, wr_ref[...], preferred_element_type=F32, precision=HIGHEST)
    lane = lax.broadcasted_iota(jnp.int32, logits.shape, 1)
    neg = jnp.float32(-jnp.inf)
    logits = jnp.where(lane < N_EXPERTS, logits, neg)
    v1 = jnp.max(logits, axis=-1, keepdims=True)
    i1 = jnp.min(jnp.where(logits == v1, lane, 128), axis=-1, keepdims=True)
    rest = jnp.where(lane == i1, neg, logits)
    v2 = jnp.max(rest, axis=-1, keepdims=True)
    i2 = jnp.min(jnp.where(rest == v2, lane, 128), axis=-1, keepdims=True)
    e = jnp.exp(v2 - v1)
    w1 = 1.0 / (1.0 + e)
    w2 = e / (1.0 + e)
    out_lane = lax.broadcasted_iota(jnp.int32, (TM, 8), 1)
    r = jnp.where(out_lane == 0, i1.astype(F32), 0.0)
    r = jnp.where(out_lane == 1, i2.astype(F32), r)
    r = jnp.where(out_lane == 2, w1, r)
    r = jnp.where(out_lane == 3, w2, r)
    r_out[...] = r


def _expert_kernel(meta_ref, src_ref, h_hbm, wgu_ref, wd_ref, y_ref, hbuf, sem):
    t = pl.program_id(0)
    n_tiles = pl.num_programs(0)

    def row_copy(r, src_row):
        return pltpu.make_async_copy(h_hbm.at[pl.ds(src_row, 1)], hbuf.at[pl.ds(r, 1)], sem)

    @pl.when(t < meta_ref[n_tiles])
    def _():
        def issue(r, carry):
            row_copy(r, src_ref[0, 0, r]).start()
            return carry

        lax.fori_loop(0, TME, issue, 0)

        def wait(r, carry):
            row_copy(r, 0).wait()
            return carry

        lax.fori_loop(0, TME, wait, 0)
        y_ref[...] = _swiglu_rows(hbuf[...].astype(BF16), wgu_ref, wd_ref)

    @pl.when(t >= meta_ref[n_tiles])
    def _():
        y_ref[...] = jnp.zeros_like(y_ref)


def _combine_kernel(pos_ref, y_hbm, x_ref, mod_ref, r_ref, o_ref, ybuf, sem):
    def row_copy(k, r, src_row):
        return pltpu.make_async_copy(y_hbm.at[pl.ds(src_row, 1)], ybuf.at[k, pl.ds(r, 1)], sem)

    def issue(r, carry):
        row_copy(0, r, pos_ref[0, 0, r]).start()
        row_copy(1, r, pos_ref[0, 1, r]).start()
        return carry

    lax.fori_loop(0, TM, issue, 0)

    def wait(r, carry):
        row_copy(0, r, 0).wait()
        row_copy(1, r, 0).wait()
        return carry

    lax.fori_loop(0, TM, wait, 0)
    r = r_ref[...]
    out = r[:, 2:3] * ybuf[0] + r[:, 3:4] * ybuf[1]
    o_ref[...] = x_ref[...] + mod_ref[5:6, :] * out


def _final_norm_kernel(x_ref, g_ref, o_ref):
    x = x_ref[...]
    o_ref[...] = (x * lax.rsqrt(jnp.mean(x * x, axis=-1, keepdims=True) + EPS)) * g_ref[...]


def _moe_ffn(x, mods, norm_g, w_router, w_gu, w_down, n_ctx):
    B, T, _ = x.shape
    nt = T // TM
    n_tok = B * T
    wr = jnp.zeros((D, 128), F32).at[:, :N_EXPERTS].set(w_router)
    h, route = pl.pallas_call(
        _router_kernel,
        out_shape=(_sds((B, T, D), F32), _sds((B, T, 8), F32)),
        grid=(B, nt),
        in_specs=[_x_spec(), _mod_spec(n_ctx // TM), _const_spec((1, D)), _const_spec((D, 128))],
        out_specs=(_x_spec(), pl.BlockSpec((None, TM, 8), lambda b, i: (b, i, 0))),
        compiler_params=_params(("parallel", "parallel"), 32),
        name="router",
    )(x, mods, norm_g.reshape(1, D), wr)

    eid = route.reshape(n_tok, 8)[:, :2].astype(jnp.int32).reshape(-1)
    onehot = (eid[:, None] == jnp.arange(N_EXPERTS, dtype=jnp.int32)[None, :]).astype(jnp.int32)
    rank = jnp.sum((jnp.cumsum(onehot, axis=0) - onehot) * onehot, axis=1)
    counts = jnp.sum(onehot, axis=0)
    padded = ((counts + TME - 1) // TME) * TME
    ends = jnp.cumsum(padded)
    starts = ends - padded
    pos = starts[eid] + rank
    n_rows = 2 * n_tok + N_EXPERTS * TME
    n_tiles = n_rows // TME
    src = jnp.zeros((n_rows,), jnp.int32).at[pos].set(jnp.arange(2 * n_tok, dtype=jnp.int32) // 2)
    tile_exp = jnp.minimum(jnp.searchsorted(ends, jnp.arange(n_tiles, dtype=jnp.int32) * TME, side="right"),
                           N_EXPERTS - 1).astype(jnp.int32)
    meta = jnp.concatenate([tile_exp, (ends[-1:] // TME).astype(jnp.int32)])

    y = pl.pallas_call(
        _expert_kernel,
        out_shape=_sds((n_rows, D), F32),
        grid_spec=pltpu.PrefetchScalarGridSpec(
            num_scalar_prefetch=1,
            grid=(n_tiles,),
            in_specs=[pl.BlockSpec((1, 1, TME), lambda t, meta: (t, 0, 0), memory_space=pltpu.SMEM),
                      pl.BlockSpec(memory_space=pl.ANY),
                      pl.BlockSpec((None, D, 2 * D_FF), lambda t, meta: (meta[t], 0, 0)),
                      pl.BlockSpec((None, D_FF, D), lambda t, meta: (meta[t], 0, 0))],
            out_specs=pl.BlockSpec((TME, D), lambda t, meta: (t, 0)),
            scratch_shapes=[pltpu.VMEM((TME, D), F32), pltpu.SemaphoreType.DMA(())]),
        compiler_params=_params(("arbitrary",), 56),
        name="experts",
    )(meta, src.reshape(n_tiles, 1, TME), h.reshape(n_tok, D), w_gu.astype(BF16), w_down.astype(BF16))

    pos_t = pos.reshape(B * nt, TM, 2).transpose(0, 2, 1)
    return pl.pallas_call(
        _combine_kernel,
        out_shape=_sds((B, T, D), F32),
        grid=(B, nt),
        in_specs=[pl.BlockSpec((1, 2, TM), lambda b, i: (b * nt + i, 0, 0), memory_space=pltpu.SMEM),
                  pl.BlockSpec(memory_space=pl.ANY),
                  _x_spec(), _mod_spec(n_ctx // TM),
                  pl.BlockSpec((None, TM, 8), lambda b, i: (b, i, 0))],
        out_specs=_x_spec(),
        scratch_shapes=[pltpu.VMEM((2, TM, D), F32), pltpu.SemaphoreType.DMA(())],
        compiler_params=_params(("arbitrary", "arbitrary"), 32),
        name="moe_combine",
    )(pos_t, y, x, mods, route)


def kernel(x, c, ctx, c_ctx, l0_ada_w, l0_ada_b, l0_norm_mix, l0_norm_ffn, l0_conv_w_in, l0_conv_w, l0_conv_w_out, l0_ffn_w_gu, l0_ffn_w_down, l1_ada_w, l1_ada_b, l1_norm_mix, l1_norm_ffn, l1_attn_w_qkv, l1_attn_lam, l1_attn_subln, l1_attn_w_o, l1_moe_router, l1_moe_w_gu, l1_moe_w_down, l2_ada_w, l2_ada_b, l2_norm_mix, l2_norm_ffn, l2_ssm_a_re, l2_ssm_a_im, l2_ssm_log_dt, l2_ssm_b_re, l2_ssm_b_im, l2_ssm_c_re, l2_ssm_c_im, l2_ssm_d, l2_ssm_w_glu, l2_ffn_w_gu, l2_ffn_w_down, l3_ada_w, l3_ada_b, l3_norm_mix, l3_norm_ffn, l3_conv_w_in, l3_conv_w, l3_conv_w_out, l3_moe_router, l3_moe_w_gu, l3_moe_w_down, final_norm):
    B, n_lat, _ = x.shape
    n_ctx = ctx.shape[1]
    T = n_ctx + n_lat
    assert n_ctx % TM == 0 and n_lat % TM == 0 and n_lat % GRID_W == 0 and B < 8
    xs = jnp.concatenate([ctx, x], axis=1)
    cond8 = jnp.concatenate([c, c_ctx[None], jnp.zeros((8 - B - 1, D), F32)], axis=0)

    def mods_of(w, b):
        m = _ada(cond8, w, b).reshape(8, 6, D)
        return jnp.stack([jnp.broadcast_to(m[B], (B, 6, D)), m[:B]], axis=1)

    mods = mods_of(l0_ada_w, l0_ada_b)
    xs = _conv_mixer(xs, mods, l0_norm_mix, l0_conv_w_in, l0_conv_w, l0_conv_w_out, n_ctx)
    xs = _dense_ffn(xs, mods, l0_norm_ffn, l0_ffn_w_gu, l0_ffn_w_down, n_ctx)

    mods = mods_of(l1_ada_w, l1_ada_b)
    xs = _attn_mixer(xs, mods, l1_norm_mix, l1_attn_w_qkv, l1_attn_lam, l1_attn_subln, l1_attn_w_o, n_ctx,
                     0.8 - 0.6 * math.exp(-0.3 * 1))
    xs = _moe_ffn(xs, mods, l1_norm_ffn, l1_moe_router, l1_moe_w_gu, l1_moe_w_down, n_ctx)

    mods = mods_of(l2_ada_w, l2_ada_b)
    xs = _s5_mixer(xs, mods, l2_norm_mix,
                   (l2_ssm_a_re, l2_ssm_a_im, l2_ssm_log_dt, l2_ssm_b_re, l2_ssm_b_im, l2_ssm_c_re, l2_ssm_c_im,
                    l2_ssm_d, l2_ssm_w_glu), n_ctx)
    xs = _dense_ffn(xs, mods, l2_norm_ffn, l2_ffn_w_gu, l2_ffn_w_down, n_ctx)

    mods = mods_of(l3_ada_w, l3_ada_b)
    xs = _conv_mixer(xs, mods, l3_norm_mix, l3_conv_w_in, l3_conv_w, l3_conv_w_out, n_ctx)
    xs = _moe_ffn(xs, mods, l3_norm_ffn, l3_moe_router, l3_moe_w_gu, l3_moe_w_down, n_ctx)

    nct = n_ctx // TM
    return pl.pallas_call(
        _final_norm_kernel,
        out_shape=_sds((B, n_lat, D), F32),
        grid=(B, n_lat // TM),
        in_specs=[pl.BlockSpec((None, TM, D), lambda b, i: (b, i + nct, 0)), _const_spec((1, D))],
        out_specs=_x_spec(),
        compiler_params=_params(("parallel", "parallel"), 32),
        name="final_norm",
    )(xs, final_norm.reshape(1, D))
```

```python
import functools
import math

import jax
import jax.numpy as jnp
from jax import lax
from jax.experimental import pallas as pl
from jax.experimental.pallas import tpu as pltpu

F32 = jnp.float32
BF16 = jnp.bfloat16
HIGHEST = lax.Precision.HIGHEST

D = 1024
GRID_W = 64
N_HEADS = 8
HEAD_DIM = 64
V_DIM = 128
ROPE_BASE = 10000.0
GROUP_SIZE = 16
N_GROUPS = 64
STATE_DIM = 64
D_FF = 2816
N_EXPERTS = 8
EPS = 1e-6

TM = 256
FF_CHUNK = 256
TK = 256
S5_T = 16
TME = 256
MIB = 1 << 20


def _sds(shape, dtype):
    return jax.ShapeDtypeStruct(shape, dtype)


def _params(sem, vmem_mib):
    return pltpu.CompilerParams(dimension_semantics=sem, vmem_limit_bytes=vmem_mib * MIB)


def _modulate(x, g, shift, scale):
    y = x * lax.rsqrt(jnp.mean(x * x, axis=-1, keepdims=True) + EPS)
    return (y * g) * (1.0 + scale) + shift


def _x_spec():
    return pl.BlockSpec((None, TM, D), lambda b, i: (b, i, 0))


def _mod_spec(n_ctx_tiles):
    return pl.BlockSpec((None, None, 6, D), lambda b, i: (b, jnp.where(i >= n_ctx_tiles, 1, 0), 0, 0))


def _const_spec(shape):
    nd = len(shape)
    return pl.BlockSpec(shape, lambda b, i: (0,) * nd)


def _ada_kernel(c_ref, w_ref, b_ref, o_ref):
    c = c_ref[...]
    s = c * jax.nn.sigmoid(c)
    o_ref[...] = jnp.dot(s, w_ref[...], preferred_element_type=F32, precision=HIGHEST) + b_ref[...]


def _ada(cond8, w, b):
    nb = 4
    bn = w.shape[1] // nb
    return pl.pallas_call(
        _ada_kernel,
        out_shape=_sds((8, w.shape[1]), F32),
        grid=(nb,),
        in_specs=[pl.BlockSpec((8, D), lambda j: (0, 0)),
                  pl.BlockSpec((D, bn), lambda j: (0, j)),
                  pl.BlockSpec((1, bn), lambda j: (0, j))],
        out_specs=pl.BlockSpec((8, bn), lambda j: (0, j)),
        compiler_params=_params(("parallel",), 40),
        name="ada",
    )(cond8, w, b.reshape(1, -1))


def _conv_in_kernel(x_ref, mod_ref, g_ref, w_ref, b_out, z_out):
    h = _modulate(x_ref[...], g_ref[...], mod_ref[0:1, :], mod_ref[1:2, :]).astype(BF16)
    b_out[...] = jnp.dot(h, w_ref[:, 0:D], preferred_element_type=F32).astype(BF16)
    c_gate = jnp.dot(h, w_ref[:, D:2 * D], preferred_element_type=F32)
    v = jnp.dot(h, w_ref[:, 2 * D:3 * D], preferred_element_type=F32)
    z_out[...] = c_gate * v


def _conv_out_kernel(z_ref, zp_ref, zn_ref, b_ref, cw_ref, w_ref, x_ref, mod_ref, o_ref, *, n_ctx, n_tot):
    i = pl.program_id(1)
    z = z_ref[...]
    row = lax.broadcasted_iota(jnp.int32, (TM, 1), 0)
    grow = row + i * TM
    prev_row = zp_ref[7:8, :]
    next_row = zn_ref[0:1, :]
    zm1 = jnp.where(row == 0, prev_row, pltpu.roll(z, 1, axis=0))
    zp1 = jnp.where(row == TM - 1, next_row, pltpu.roll(z, TM - 1, axis=0))
    zm1 = jnp.where((grow == 0) | (grow == n_ctx), 0.0, zm1)
    zp1 = jnp.where((grow == n_ctx - 1) | (grow == n_tot - 1), 0.0, zp1)
    y = cw_ref[0:1, :] * zm1 + cw_ref[1:2, :] * z + cw_ref[2:3, :] * zp1
    a = (b_ref[...].astype(F32) * y).astype(BF16)
    o_ref[...] = x_ref[...] + mod_ref[2:3, :] * jnp.dot(a, w_ref[...], preferred_element_type=F32)


def _conv_mixer(x, mods, norm_g, w_in, conv_w, w_out, n_ctx):
    B, T, _ = x.shape
    nt = T // TM
    nct = n_ctx // TM
    b_gate, z = pl.pallas_call(
        _conv_in_kernel,
        out_shape=(_sds((B, T, D), BF16), _sds((B, T, D), F32)),
        grid=(B, nt),
        in_specs=[_x_spec(), _mod_spec(nct), _const_spec((1, D)), _const_spec((D, 3 * D))],
        out_specs=(_x_spec(), _x_spec()),
        compiler_params=_params(("parallel", "parallel"), 48),
        name="conv_in",
    )(x, mods, norm_g.reshape(1, D), w_in.astype(BF16))
    r8 = TM // 8
    return pl.pallas_call(
        functools.partial(_conv_out_kernel, n_ctx=n_ctx, n_tot=T),
        out_shape=_sds((B, T, D), F32),
        grid=(B, nt),
        in_specs=[_x_spec(),
                  pl.BlockSpec((None, 8, D), lambda b, i: (b, jnp.maximum(i * r8 - 1, 0), 0)),
                  pl.BlockSpec((None, 8, D), lambda b, i: (b, jnp.minimum((i + 1) * r8, T // 8 - 1), 0)),
                  _x_spec(), _const_spec((3, D)), _const_spec((D, D)), _x_spec(), _mod_spec(nct)],
        out_specs=_x_spec(),
        compiler_params=_params(("parallel", "parallel"), 48),
        name="conv_out",
    )(z, z, z, b_gate, conv_w, w_out.astype(BF16), x, mods)


def _qkv_kernel(x_ref, mod_ref, g_ref, w_ref, cos_ref, slo_ref, shi_ref, qt_out, k_out, vt_out):
    h = _modulate(x_ref[...], g_ref[...], mod_ref[0:1, :], mod_ref[1:2, :]).astype(BF16)
    cos = cos_ref[...]
    s_lo = slo_ref[...]
    s_hi = shi_ref[...]

    def rope(t):
        return t * cos + pltpu.roll(t, 128 - 16, axis=1) * s_lo + pltpu.roll(t, 16, axis=1) * s_hi

    q_scale = (HEAD_DIM ** -0.5) * math.log2(math.e)
    for hd in range(N_HEADS):
        lo, hi = hd * 128, (hd + 1) * 128
        q = jnp.dot(h, w_ref[:, lo:hi], preferred_element_type=F32) * q_scale
        qt_out[hd] = rope(q).T.astype(BF16)
        k = jnp.dot(h, w_ref[:, D + lo:D + hi], preferred_element_type=F32)
        k_out[:, lo:hi] = rope(k).astype(BF16)
        v = jnp.dot(h, w_ref[:, 2 * D + lo:2 * D + hi], preferred_element_type=F32)
        vt_out[hd] = v.astype(BF16).astype(F32).T.astype(BF16)


def _attn_kernel(qt_ref, k_ref, vt_ref, lam_ref, sub_ref, o_ref, s_buf, acc_ref, *,
                 n_ctx_tiles, n_lat_chunks, lam_init):
    i = pl.program_id(2)
    qt = qt_ref[...]
    row = lax.broadcasted_iota(jnp.int32, qt.shape, 0)
    zero = jnp.zeros_like(qt)
    q_cat = jnp.concatenate([jnp.where(row < HEAD_DIM, qt, zero), jnp.where(row >= HEAD_DIM, qt, zero)], axis=1)

    def scores(first_tile, n_tiles):
        start = first_tile * TM
        if not isinstance(first_tile, int):
            start = pl.multiple_of(start, TM)
        return jnp.dot(k_ref[pl.ds(start, n_tiles * TM), :], q_cat, preferred_element_type=F32)

    def absorb(s, first_tile, n_tiles, m, l):
        m_new = jnp.maximum(m, jnp.max(s, axis=0, keepdims=True))
        p = jnp.exp2(s - m_new)
        alpha = jnp.exp2(m - m_new)
        l_new = alpha * l + jnp.sum(p, axis=0, keepdims=True)
        pb = p.astype(BF16)
        pv = jnp.dot(vt_ref[first_tile], pb[0:TM, :], preferred_element_type=F32)
        for c in range(1, n_tiles):
            pv = pv + jnp.dot(vt_ref[first_tile + c], pb[c * TM:(c + 1) * TM, :], preferred_element_type=F32)
        acc_ref[...] = alpha * acc_ref[...] + pv
        return m_new, l_new

    def finish(l):
        acc = acc_ref[...]
        lv = lam_ref[...]
        lam = (jnp.exp(jnp.sum(lv[0:1, :] * lv[1:2, :], axis=-1, keepdims=True))
               - jnp.exp(jnp.sum(lv[2:3, :] * lv[3:4, :], axis=-1, keepdims=True)) + lam_init)
        o = (acc[:, 0:TM] / l[:, 0:TM] - lam * (acc[:, TM:2 * TM] / l[:, TM:2 * TM])).T
        o = o * lax.rsqrt(jnp.mean(o * o, axis=-1, keepdims=True) + EPS)
        o_ref[...] = ((o * sub_ref[...]) * (1.0 - lam_init)).astype(BF16)

    acc_ref[...] = jnp.zeros_like(acc_ref)
    m_ctx, l_ctx = absorb(scores(0, n_ctx_tiles), 0, n_ctx_tiles,
                          jnp.full((1, 2 * TM), -1e30, F32), jnp.zeros((1, 2 * TM), F32))

    @pl.when(i < n_ctx_tiles)
    def _():
        finish(l_ctx)

    @pl.when(i >= n_ctx_tiles)
    def _():
        last = n_ctx_tiles + 2 * (n_lat_chunks - 1)
        s_buf[0] = scores(n_ctx_tiles, 2)

        def two_chunks(t, carry):
            m, l = carry
            first = n_ctx_tiles + 4 * t
            s_buf[1] = scores(first + 2, 2)
            m, l = absorb(s_buf[0], first, 2, m, l)
            s_buf[0] = scores(jnp.minimum(first + 4, last), 2)
            return absorb(s_buf[1], first + 2, 2, m, l)

        _, l_all = lax.fori_loop(0, n_lat_chunks // 2, two_chunks, (m_ctx, l_ctx))
        finish(l_all)


def _out_proj_kernel(a_ref, w_ref, x_ref, mod_ref, o_ref):
    o_ref[...] = x_ref[...] + mod_ref[2:3, :] * jnp.dot(a_ref[...], w_ref[...], preferred_element_type=F32)


def _rope_tables(n_ctx, n_lat):
    t = jnp.arange(n_lat, dtype=jnp.int32)
    row = (t // GRID_W).astype(F32)
    col = (t % GRID_W).astype(F32)
    n_freq = HEAD_DIM // 4
    inv = ROPE_BASE ** (-jnp.arange(n_freq, dtype=F32) / n_freq)
    ang_r = row[:, None] * inv
    ang_c = col[:, None] * inv
    zero = jnp.zeros_like(ang_r)
    cos64 = jnp.concatenate([jnp.cos(ang_r), jnp.cos(ang_r), jnp.cos(ang_c), jnp.cos(ang_c)], axis=-1)
    slo64 = jnp.concatenate([-jnp.sin(ang_r), zero, -jnp.sin(ang_c), zero], axis=-1)
    shi64 = jnp.concatenate([zero, jnp.sin(ang_r), zero, jnp.sin(ang_c)], axis=-1)

    def full(tab, ctx_val):
        tab = jnp.concatenate([tab, tab], axis=-1)
        return jnp.concatenate([jnp.full((n_ctx, 128), ctx_val, F32), tab], axis=0)

    return full(cos64, 1.0), full(slo64, 0.0), full(shi64, 0.0)


def _attn_mixer(x, mods, norm_g, w_qkv, lam_vecs, subln, w_o, n_ctx, lam_init):
    B, T, _ = x.shape
    nt = T // TM
    nct = n_ctx // TM
    cos, s_lo, s_hi = _rope_tables(n_ctx, T - n_ctx)
    tab_spec = pl.BlockSpec((TM, 128), lambda b, i: (i, 0))
    assert (T - n_ctx) % (4 * TM) == 0
    t_spec = pl.BlockSpec((None, N_HEADS, None, 128, TM), lambda b, i: (b, 0, i, 0, 0))
    t_shape = _sds((B, N_HEADS, nt, 128, TM), BF16)
    qt, k, vt = pl.pallas_call(
        _qkv_kernel,
        out_shape=(t_shape, _sds((B, T, D), BF16), t_shape),
        grid=(B, nt),
        in_specs=[_x_spec(), _mod_spec(nct), _const_spec((1, D)), _const_spec((D, 3 * D)),
                  tab_spec, tab_spec, tab_spec],
        out_specs=(t_spec, _x_spec(), t_spec),
        compiler_params=_params(("parallel", "parallel"), 48),
        name="qkv",
    )(x, mods, norm_g.reshape(1, D), w_qkv.astype(BF16), cos, s_lo, s_hi)

    attn = pl.pallas_call(
        functools.partial(_attn_kernel, n_ctx_tiles=nct, n_lat_chunks=(T - n_ctx) // (2 * TM), lam_init=lam_init),
        out_shape=_sds((B, T, D), BF16),
        grid=(B, N_HEADS, nt),
        in_specs=[pl.BlockSpec((None, None, None, 128, TM), lambda b, h, i: (b, h, i, 0, 0)),
                  pl.BlockSpec((None, T, 128), lambda b, h, i: (b, 0, h)),
                  pl.BlockSpec((None, None, nt, 128, TM), lambda b, h, i: (b, h, 0, 0, 0)),
                  pl.BlockSpec((4, HEAD_DIM), lambda b, h, i: (0, 0)),
                  pl.BlockSpec((1, V_DIM), lambda b, h, i: (0, 0))],
        out_specs=pl.BlockSpec((None, TM, 128), lambda b, h, i: (b, i, h)),
        scratch_shapes=[pltpu.VMEM((2, 2 * TM, 2 * TM), F32), pltpu.VMEM((V_DIM, 2 * TM), F32)],
        compiler_params=_params(("parallel", "parallel", "parallel"), 48),
        name="diff_attn",
    )(qt, k, vt, lam_vecs, subln.reshape(1, V_DIM))

    return pl.pallas_call(
        _out_proj_kernel,
        out_shape=_sds((B, T, D), F32),
        grid=(B, nt),
        in_specs=[_x_spec(), _const_spec((D, D)), _x_spec(), _mod_spec(nct)],
        out_specs=_x_spec(),
        compiler_params=_params(("parallel", "parallel"), 48),
        name="attn_out",
    )(attn, w_o.astype(BF16), x, mods)


def _mod_only_kernel(x_ref, mod_ref, g_ref, h_out):
    h_out[...] = _modulate(x_ref[...], g_ref[...], mod_ref[0:1, :], mod_ref[1:2, :])


def _s5_chunk_rows(h_ref):
    return jnp.concatenate([h_ref[:, s, :] for s in range(S5_T)], axis=-1).astype(BF16)


def _s5_state_kernel(h_ref, w_ref, sfr, sfi, sbr, sbi):
    s = jnp.dot(_s5_chunk_rows(h_ref), w_ref[...], preferred_element_type=F32)
    sfr[...] = s[:, 0:512]
    sfi[...] = s[:, 512:1024]
    sbr[...] = s[:, 1024:1536]
    sbi[...] = s[:, 1536:2048]


def _s5_scan_kernel(sfr, sfi, sbr, sbi, afr, afi, abr, abi, hfr, hfi, hbr, hbi, *, n_ctx_chunks, n_chunks):
    ar, ai = afr[...], afi[...]

    def fwd(n, carry):
        hr, hi = carry
        hfr[n] = hr
        hfi[n] = hi
        sr, si = sfr[n], sfi[n]
        return ar * hr - ai * hi + sr, ar * hi + ai * hr + si

    zero = jnp.zeros(ar.shape, F32)
    lax.fori_loop(0, n_chunks, fwd, (zero, zero))

    br, bi = abr[...], abi[...]

    def bwd(t, carry, top):
        n = top - t
        hr, hi = carry
        hbr[n] = hr
        hbi[n] = hi
        sr, si = sbr[n], sbi[n]
        return br * hr - bi * hi + sr, br * hi + bi * hr + si

    carry = lax.fori_loop(0, n_ctx_chunks, functools.partial(bwd, top=n_ctx_chunks - 1), (zero, zero))
    lax.fori_loop(0, n_chunks - n_ctx_chunks, functools.partial(bwd, top=n_chunks - 1), carry)


def _s5_out_kernel(h_ref, m_ref, hfr, hfi, hbr, hbi, v_ref, y_ref):
    state = jnp.concatenate([hfr[...], hfi[...], hbr[...], hbi[...]], axis=-1).astype(BF16)
    y = (jnp.dot(_s5_chunk_rows(h_ref), m_ref[...], preferred_element_type=F32)
         + jnp.dot(state, v_ref[...], preferred_element_type=F32))
    for t in range(S5_T // 2):
        y_ref[:, t, :] = y[:, t * 128:(t + 1) * 128]


def _glu_kernel(y_ref, x_ref, mod_ref, g_ref, d_ref, w_ref, o_ref):
    x = x_ref[...]
    hl = _modulate(x, g_ref[...], mod_ref[0:1, :], mod_ref[1:2, :])
    y = y_ref[...] + d_ref[...] * hl
    a = jax.nn.gelu(y).astype(BF16)
    val = jnp.dot(a, w_ref[:, 0:D], preferred_element_type=F32)
    gate = jnp.dot(a, w_ref[:, D:2 * D], preferred_element_type=F32)
    o_ref[...] = x + mod_ref[2:3, :] * (val * jax.nn.sigmoid(gate))


def _s5_matrices(a_re, a_im, log_dt, b_re, b_im, c_re, c_im):
    T = S5_T
    hp = functools.partial(jnp.einsum, precision=HIGHEST)
    pw_re, pw_im, bb_re, bb_im = [], [], [], []
    for d in range(2):
        ar, ai = a_re[d].astype(F32), a_im[d].astype(F32)
        dt = jnp.exp(log_dt[d].astype(F32))[:, None]
        mag = jnp.exp(dt * ar)
        ab_re = mag * jnp.cos(dt * ai)
        ab_im = mag * jnp.sin(dt * ai)
        den = ar * ar + ai * ai
        nr = ab_re - 1.0
        co_re = (nr * ar + ab_im * ai) / den
        co_im = (ab_im * ar - nr * ai) / den
        br, bi = b_re[d].astype(F32), b_im[d].astype(F32)
        bb_re.append(co_re[..., None] * br - co_im[..., None] * bi)
        bb_im.append(co_re[..., None] * bi + co_im[..., None] * br)
        pr = [jnp.ones_like(ab_re)]
        pi = [jnp.zeros_like(ab_re)]
        for _ in range(T):
            pr.append(pr[-1] * ab_re - pi[-1] * ab_im)
            pi.append(pr[-2] * ab_im + pi[-1] * ab_re)
        pw_re.append(jnp.stack(pr))
        pw_im.append(jnp.stack(pi))

    def lag_kernels(d):
        mr = pw_re[d][:T, :, :, None] * bb_re[d][None] - pw_im[d][:T, :, :, None] * bb_im[d][None]
        mi = pw_re[d][:T, :, :, None] * bb_im[d][None] + pw_im[d][:T, :, :, None] * bb_re[d][None]
        return hp('gop,kgpi->kgoi', c_re[d].astype(F32), mr) - hp('gop,kgpi->kgoi', c_im[d].astype(F32), mi)

    kf, kb = lag_kernels(0), lag_kernels(1)
    lag0 = (kf[0] + kb[0])[None]
    kfull = jnp.concatenate([kb[:0:-1], lag0, kf[1:]], axis=0)
    tt = jnp.arange(T)
    idx = tt[None, :] - tt[:, None] + (T - 1)
    mt = kfull[idx]
    mt = mt.transpose(2, 0, 4, 1, 3).reshape(N_GROUPS, T * GROUP_SIZE, T * GROUP_SIZE)

    def end_state(d, powers):
        pr, pi = pw_re[d][powers], pw_im[d][powers]
        wr = pr[..., None] * bb_re[d][None] - pi[..., None] * bb_im[d][None]
        wi = pr[..., None] * bb_im[d][None] + pi[..., None] * bb_re[d][None]
        f = lambda w: w.transpose(1, 0, 3, 2).reshape(N_GROUPS, T * GROUP_SIZE, STATE_DIM)
        return f(wr), f(wi)

    def read_out(d, powers):
        pr, pi = pw_re[d][powers], pw_im[d][powers]
        cr, ci = c_re[d].astype(F32), c_im[d].astype(F32)
        qr = cr[None] * pr[:, :, None, :] - ci[None] * pi[:, :, None, :]
        qi = cr[None] * pi[:, :, None, :] + ci[None] * pr[:, :, None, :]
        f = lambda q: q.transpose(1, 3, 0, 2).reshape(N_GROUPS, STATE_DIM, T * GROUP_SIZE)
        return f(qr), f(-qi)

    wfr, wfi = end_state(0, T - 1 - tt)
    wbr, wbi = end_state(1, tt)
    vfr, vfi = read_out(0, tt + 1)
    vbr, vbi = read_out(1, T - tt)

    eye = jnp.eye(8, dtype=BF16)[None, None, :, None, None, :, None]
    nb = N_GROUPS // 8

    def spread(a6):
        a = a6.astype(BF16).transpose(0, 2, 1, 3, 4, 5)
        return a[:, :, :, :, :, None, :] * eye

    m8 = spread(mt.reshape(nb, 8, T, GROUP_SIZE, T, GROUP_SIZE)).reshape(nb, T * 128, T * 128)
    w_cat = jnp.stack([wfr, wfi, wbr, wbi], axis=2)
    w8 = spread(w_cat.reshape(nb, 8, T, GROUP_SIZE, 4, STATE_DIM)).reshape(nb, T * 128, 4 * 8 * STATE_DIM)
    v_cat = jnp.stack([vfr, vfi, vbr, vbi], axis=1)
    v8 = spread(v_cat.reshape(nb, 8, 4, STATE_DIM, T, GROUP_SIZE)).reshape(nb, 4 * 8 * STATE_DIM, T * 128)
    a_pow = [t[T].reshape(N_GROUPS // 2, 2 * STATE_DIM) for t in (pw_re[0], pw_im[0], pw_re[1], pw_im[1])]
    return m8, w8, v8, a_pow


def _s5_mixer(x, mods, norm_g, ssm, n_ctx):
    a_re, a_im, log_dt, b_re, b_im, c_re, c_im, d_skip, w_glu = ssm
    B, T, _ = x.shape
    nt = T // TM
    nct = n_ctx // TM
    nch = T // S5_T
    nb = N_GROUPS // 8
    sw = 8 * STATE_DIM
    cw = S5_T * 128
    m8, w8, v8, a_pow = _s5_matrices(a_re, a_im, log_dt, b_re, b_im, c_re, c_im)

    h = pl.pallas_call(
        _mod_only_kernel,
        out_shape=_sds((B, T, D), F32),
        grid=(B, nt),
        in_specs=[_x_spec(), _mod_spec(nct), _const_spec((1, D))],
        out_specs=_x_spec(),
        compiler_params=_params(("parallel", "parallel"), 32),
        name="s5_mod",
    )(x, mods, norm_g.reshape(1, D))
    h4 = h.reshape(B * nch, S5_T, D)

    rows = B * nch
    state_shape = _sds((rows, nb * sw), F32)
    s_parts = pl.pallas_call(
        _s5_state_kernel,
        out_shape=(state_shape,) * 4,
        grid=(nb, B),
        in_specs=[pl.BlockSpec((nch, S5_T, 128), lambda g, b: (b, 0, g)),
                  pl.BlockSpec((None, cw, 4 * sw), lambda g, b: (g, 0, 0))],
        out_specs=(pl.BlockSpec((nch, sw), lambda g, b: (b, g)),) * 4,
        compiler_params=_params(("parallel", "parallel"), 56),
        name="s5_state",
    )(h4, w8)

    n_pairs = N_GROUPS // 2
    npb = 8
    view = lambda t: t.reshape(B, nch, n_pairs, 128)
    scan_blk = pl.BlockSpec((None, nch, npb, 128), lambda b, j: (b, 0, j, 0))
    coef_blk = pl.BlockSpec((npb, 128), lambda b, j: (j, 0))
    h_parts = pl.pallas_call(
        functools.partial(_s5_scan_kernel, n_ctx_chunks=n_ctx // S5_T, n_chunks=nch),
        out_shape=(_sds((B, nch, n_pairs, 128), F32),) * 4,
        grid=(B, n_pairs // npb),
        in_specs=[scan_blk] * 4 + [coef_blk] * 4,
        out_specs=(scan_blk,) * 4,
        compiler_params=_params(("parallel", "parallel"), 56),
        name="s5_scan",
    )(*[view(t) for t in s_parts], *a_pow)

    half = S5_T // 2
    y4 = pl.pallas_call(
        _s5_out_kernel,
        out_shape=_sds((rows, S5_T, D), F32),
        grid=(nb, 2, B),
        in_specs=[pl.BlockSpec((nch, S5_T, 128), lambda g, hf, b: (b, 0, g)),
                  pl.BlockSpec((None, cw, half * 128), lambda g, hf, b: (g, 0, hf))]
                 + [pl.BlockSpec((nch, sw), lambda g, hf, b: (b, g))] * 4
                 + [pl.BlockSpec((None, 4 * sw, half * 128), lambda g, hf, b: (g, 0, hf))],
        out_specs=pl.BlockSpec((nch, half, 128), lambda g, hf, b: (b, hf, g)),
        compiler_params=_params(("parallel", "parallel", "parallel"), 56),
        name="s5_out",
    )(h4, m8, *[t.reshape(rows, nb * sw) for t in h_parts], v8)
    y = y4.reshape(B, T, D)

    return pl.pallas_call(
        _glu_kernel,
        out_shape=_sds((B, T, D), F32),
        grid=(B, nt),
        in_specs=[_x_spec(), _x_spec(), _mod_spec(nct), _const_spec((1, D)), _const_spec((1, D)),
                  _const_spec((D, 2 * D))],
        out_specs=_x_spec(),
        compiler_params=_params(("parallel", "parallel"), 48),
        name="s5_glu",
    )(y, x, mods, norm_g.reshape(1, D), d_skip.reshape(1, D), w_glu.astype(BF16))


def _swiglu_rows(h, wgu_ref, wd_ref):
    acc = jnp.zeros((h.shape[0], D), F32)
    for c in range(D_FF // FF_CHUNK):
        lo, hi = c * FF_CHUNK, (c + 1) * FF_CHUNK
        g = jnp.dot(h, wgu_ref[:, lo:hi], preferred_element_type=F32)
        u = jnp.dot(h, wgu_ref[:, D_FF + lo:D_FF + hi], preferred_element_type=F32)
        a = ((g * jax.nn.sigmoid(g)) * u).astype(BF16)
        acc = acc + jnp.dot(a, wd_ref[lo:hi, :], preferred_element_type=F32)
    return acc


def _ffn_kernel(x_ref, mod_ref, g_ref, wgu_ref, wd_ref, o_ref):
    x = x_ref[...]
    h = _modulate(x, g_ref[...], mod_ref[3:4, :], mod_ref[4:5, :]).astype(BF16)
    o_ref[...] = x + mod_ref[5:6, :] * _swiglu_rows(h, wgu_ref, wd_ref)


def _dense_ffn(x, mods, norm_g, w_gu, w_down, n_ctx):
    B, T, _ = x.shape
    return pl.pallas_call(
        _ffn_kernel,
        out_shape=_sds((B, T, D), F32),
        grid=(B, T // TM),
        in_specs=[_x_spec(), _mod_spec(n_ctx // TM), _const_spec((1, D)),
                  _const_spec((D, 2 * D_FF)), _const_spec((D_FF, D))],
        out_specs=_x_spec(),
        compiler_params=_params(("parallel", "parallel"), 56),
        name="dense_ffn",
    )(x, mods, norm_g.reshape(1, D), w_gu.astype(BF16), w_down.astype(BF16))


def _router_kernel(x_ref, mod_ref, g_ref, wr_ref, h_out, r_out):
    h = _modulate(x_ref[...], g_ref[...], mod_ref[3:4, :], mod_ref[4:5, :])
    h_out[...] = h
    logits = jnp.dot(h, wr_ref[...], preferred_element_type=F32, precision=HIGHEST)
    lane = lax.broadcasted_iota(jnp.int32, logits.shape, 1)
    neg = jnp.float32(-jnp.inf)
    logits = jnp.where(lane < N_EXPERTS, logits, neg)
    v1 = jnp.max(logits, axis=-1, keepdims=True)
    i1 = jnp.min(jnp.where(logits == v1, lane, 128), axis=-1, keepdims=True)
    rest = jnp.where(lane == i1, neg, logits)
    v2 = jnp.max(rest, axis=-1, keepdims=True)
    i2 = jnp.min(jnp.where(rest == v2, lane, 128), axis=-1, keepdims=True)
    e = jnp.exp(v2 - v1)
    w1 = 1.0 / (1.0 + e)
    w2 = e / (1.0 + e)
    out_lane = lax.broadcasted_iota(jnp.int32, (TM, 8), 1)
    r = jnp.where(out_lane == 0, i1.astype(F32), 0.0)
    r = jnp.where(out_lane == 1, i2.astype(F32), r)
    r = jnp.where(out_lane == 2, w1, r)
    r = jnp.where(out_lane == 3, w2, r)
    r_out[...] = r


ROW_UNROLL = 8


def _gather_rows(idx_of, src_hbm, dst, sem, n_rows, *, wait):
    def body(g, carry):
        for u in range(ROW_UNROLL):
            r = g * ROW_UNROLL + u
            if wait:
                pltpu.make_async_copy(src_hbm.at[pl.ds(0, 1)], dst.at[pl.ds(r, 1)], sem).wait()
            else:
                pltpu.make_async_copy(src_hbm.at[pl.ds(idx_of(r), 1)], dst.at[pl.ds(r, 1)], sem).start(priority=u % 2)
        return carry

    lax.fori_loop(0, n_rows // ROW_UNROLL, body, 0)


def _expert_kernel(meta_ref, src_ref, nxt_ref, h_hbm, wgu_ref, wd_ref, y_ref, hbuf, sems):
    t = pl.program_id(0)
    n_used = meta_ref[pl.num_programs(0)]
    slot = lax.rem(t, 2)

    @pl.when(t == 0)
    def _():
        _gather_rows(lambda r: src_ref[0, 0, r], h_hbm, hbuf.at[0], sems.at[0], TME, wait=False)

    @pl.when(t + 1 < n_used)
    def _():
        _gather_rows(lambda r: nxt_ref[0, 0, r], h_hbm, hbuf.at[1 - slot], sems.at[1 - slot], TME, wait=False)

    @pl.when(t < n_used)
    def _():
        _gather_rows(None, h_hbm, hbuf.at[slot], sems.at[slot], TME, wait=True)
        y_ref[...] = _swiglu_rows(hbuf[slot].astype(BF16), wgu_ref, wd_ref)

    @pl.when(t >= n_used)
    def _():
        y_ref[...] = jnp.zeros_like(y_ref)


def _combine_kernel(pos_ref, y_hbm, x_ref, mod_ref, r_ref, o_ref, ybuf, sem):
    for k in range(2):
        _gather_rows(lambda r, k=k: pos_ref[0, k, r], y_hbm, ybuf.at[k], sem, TM, wait=False)
    for k in range(2):
        _gather_rows(None, y_hbm, ybuf.at[k], sem, TM, wait=True)
    r = r_ref[...]
    out = r[:, 2:3] * ybuf[0] + r[:, 3:4] * ybuf[1]
    o_ref[...] = x_ref[...] + mod_ref[5:6, :] * out


def _final_norm_kernel(x_ref, g_ref, o_ref):
    x = x_ref[...]
    o_ref[...] = (x * lax.rsqrt(jnp.mean(x * x, axis=-1, keepdims=True) + EPS)) * g_ref[...]


def _moe_ffn(x, mods, norm_g, w_router, w_gu, w_down, n_ctx):
    B, T, _ = x.shape
    nt = T // TM
    n_tok = B * T
    wr = jnp.zeros((D, 128), F32).at[:, :N_EXPERTS].set(w_router)
    h, route = pl.pallas_call(
        _router_kernel,
        out_shape=(_sds((B, T, D), F32), _sds((B, T, 8), F32)),
        grid=(B, nt),
        in_specs=[_x_spec(), _mod_spec(n_ctx // TM), _const_spec((1, D)), _const_spec((D, 128))],
        out_specs=(_x_spec(), pl.BlockSpec((None, TM, 8), lambda b, i: (b, i, 0))),
        compiler_params=_params(("parallel", "parallel"), 32),
        name="router",
    )(x, mods, norm_g.reshape(1, D), wr)

    eid = route.reshape(n_tok, 8)[:, :2].astype(jnp.int32).reshape(-1)
    onehot = (eid[:, None] == jnp.arange(N_EXPERTS, dtype=jnp.int32)[None, :]).astype(jnp.int32)
    rank = jnp.sum((jnp.cumsum(onehot, axis=0) - onehot) * onehot, axis=1)
    counts = jnp.sum(onehot, axis=0)
    padded = ((counts + TME - 1) // TME) * TME
    ends = jnp.cumsum(padded)
    starts = ends - padded
    pos = starts[eid] + rank
    n_rows = 2 * n_tok + N_EXPERTS * TME
    n_tiles = n_rows // TME
    src = jnp.zeros((n_rows,), jnp.int32).at[pos].set(jnp.arange(2 * n_tok, dtype=jnp.int32) // 2)
    tile_row = jnp.arange(n_tiles, dtype=jnp.int32)[:, None] * TME
    tile_exp = jnp.minimum(jnp.sum((ends[None, :] <= tile_row).astype(jnp.int32), axis=1), N_EXPERTS - 1)
    meta = jnp.concatenate([tile_exp, (ends[-1:] // TME).astype(jnp.int32)])
    src3 = src.reshape(n_tiles, 1, TME)

    y = pl.pallas_call(
        _expert_kernel,
        out_shape=_sds((n_rows, D), F32),
        grid_spec=pltpu.PrefetchScalarGridSpec(
            num_scalar_prefetch=1,
            grid=(n_tiles,),
            in_specs=[pl.BlockSpec((1, 1, TME), lambda t, meta: (t, 0, 0), memory_space=pltpu.SMEM),
                      pl.BlockSpec((1, 1, TME), lambda t, meta: (jnp.minimum(t + 1, n_tiles - 1), 0, 0),
                                   memory_space=pltpu.SMEM),
                      pl.BlockSpec(memory_space=pl.ANY),
                      pl.BlockSpec((None, D, 2 * D_FF), lambda t, meta: (meta[t], 0, 0)),
                      pl.BlockSpec((None, D_FF, D), lambda t, meta: (meta[t], 0, 0))],
            out_specs=pl.BlockSpec((TME, D), lambda t, meta: (t, 0)),
            scratch_shapes=[pltpu.VMEM((2, TME, D), F32), pltpu.SemaphoreType.DMA((2,))]),
        compiler_params=_params(("arbitrary",), 56),
        name="experts",
    )(meta, src3, src3, h.reshape(n_tok, D), w_gu.astype(BF16), w_down.astype(BF16))

    pos_t = pos.reshape(B * nt, TM, 2).transpose(0, 2, 1)
    return pl.pallas_call(
        _combine_kernel,
        out_shape=_sds((B, T, D), F32),
        grid=(B, nt),
        in_specs=[pl.BlockSpec((1, 2, TM), lambda b, i: (b * nt + i, 0, 0), memory_space=pltpu.SMEM),
                  pl.BlockSpec(memory_space=pl.ANY),
                  _x_spec(), _mod_spec(n_ctx // TM),
                  pl.BlockSpec((None, TM, 8), lambda b, i: (b, i, 0))],
        out_specs=_x_spec(),
        scratch_shapes=[pltpu.VMEM((2, TM, D), F32), pltpu.SemaphoreType.DMA(())],
        compiler_params=_params(("arbitrary", "arbitrary"), 32),
        name="moe_combine",
    )(pos_t, y, x, mods, route)


def kernel(x, c, ctx, c_ctx, l0_ada_w, l0_ada_b, l0_norm_mix, l0_norm_ffn, l0_conv_w_in, l0_conv_w, l0_conv_w_out, l0_ffn_w_gu, l0_ffn_w_down, l1_ada_w, l1_ada_b, l1_norm_mix, l1_norm_ffn, l1_attn_w_qkv, l1_attn_lam, l1_attn_subln, l1_attn_w_o, l1_moe_router, l1_moe_w_gu, l1_moe_w_down, l2_ada_w, l2_ada_b, l2_norm_mix, l2_norm_ffn, l2_ssm_a_re, l2_ssm_a_im, l2_ssm_log_dt, l2_ssm_b_re, l2_ssm_b_im, l2_ssm_c_re, l2_ssm_c_im, l2_ssm_d, l2_ssm_w_glu, l2_ffn_w_gu, l2_ffn_w_down, l3_ada_w, l3_ada_b, l3_norm_mix, l3_norm_ffn, l3_conv_w_in, l3_conv_w, l3_conv_w_out, l3_moe_router, l3_moe_w_gu, l3_moe_w_down, final_norm):
    B, n_lat, _ = x.shape
    n_ctx = ctx.shape[1]
    T = n_ctx + n_lat
    assert n_ctx % TM == 0 and n_lat % TM == 0 and n_lat % GRID_W == 0 and B < 8
    xs = jnp.concatenate([ctx, x], axis=1)
    cond8 = jnp.concatenate([c, c_ctx[None], jnp.zeros((8 - B - 1, D), F32)], axis=0)

    def mods_of(w, b):
        m = _ada(cond8, w, b).reshape(8, 6, D)
        return jnp.stack([jnp.broadcast_to(m[B], (B, 6, D)), m[:B]], axis=1)

    mods = mods_of(l0_ada_w, l0_ada_b)
    xs = _conv_mixer(xs, mods, l0_norm_mix, l0_conv_w_in, l0_conv_w, l0_conv_w_out, n_ctx)
    xs = _dense_ffn(xs, mods, l0_norm_ffn, l0_ffn_w_gu, l0_ffn_w_down, n_ctx)

    mods = mods_of(l1_ada_w, l1_ada_b)
    xs = _attn_mixer(xs, mods, l1_norm_mix, l1_attn_w_qkv, l1_attn_lam, l1_attn_subln, l1_attn_w_o, n_ctx,
                     0.8 - 0.6 * math.exp(-0.3 * 1))
    xs = _moe_ffn(xs, mods, l1_norm_ffn, l1_moe_router, l1_moe_w_gu, l1_moe_w_down, n_ctx)

    mods = mods_of(l2_ada_w, l2_ada_b)
    xs = _s5_mixer(xs, mods, l2_norm_mix,
                   (l2_ssm_a_re, l2_ssm_a_im, l2_ssm_log_dt, l2_ssm_b_re, l2_ssm_b_im, l2_ssm_c_re, l2_ssm_c_im,
                    l2_ssm_d, l2_ssm_w_glu), n_ctx)
    xs = _dense_ffn(xs, mods, l2_norm_ffn, l2_ffn_w_gu, l2_ffn_w_down, n_ctx)

    mods = mods_of(l3_ada_w, l3_ada_b)
    xs = _conv_mixer(xs, mods, l3_norm_mix, l3_conv_w_in, l3_conv_w, l3_conv_w_out, n_ctx)
    xs = _moe_ffn(xs, mods, l3_norm_ffn, l3_moe_router, l3_moe_w_gu, l3_moe_w_down, n_ctx)

    nct = n_ctx // TM
    return pl.pallas_call(
        _final_norm_kernel,
        out_shape=_sds((B, n_lat, D), F32),
        grid=(B, n_lat // TM),
        in_specs=[pl.BlockSpec((None, TM, D), lambda b, i: (b, i + nct, 0)), _const_spec((1, D))],
        out_specs=_x_spec(),
        compiler_params=_params(("parallel", "parallel"), 32),
        name="final_norm",
    )(xs, final_norm.reshape(1, D))
```

```python
import functools
import math

import jax
import jax.numpy as jnp
from jax import lax
from jax.experimental import pallas as pl
from jax.experimental.pallas import tpu as pltpu

F32 = jnp.float32
BF16 = jnp.bfloat16
HIGHEST = lax.Precision.HIGHEST

D = 1024
GRID_W = 64
N_HEADS = 8
HEAD_DIM = 64
V_DIM = 128
ROPE_BASE = 10000.0
GROUP_SIZE = 16
N_GROUPS = 64
STATE_DIM = 64
D_FF = 2816
N_EXPERTS = 8
EPS = 1e-6

TM = 256
FF_CHUNK = 256
TK = 256
S5_T = 16
TME = 256
MIB = 1 << 20


def _sds(shape, dtype):
    return jax.ShapeDtypeStruct(shape, dtype)


def _params(sem, vmem_mib):
    return pltpu.CompilerParams(dimension_semantics=sem, vmem_limit_bytes=vmem_mib * MIB)


def _modulate(x, g, shift, scale):
    y = x * lax.rsqrt(jnp.mean(x * x, axis=-1, keepdims=True) + EPS)
    return (y * g) * (1.0 + scale) + shift


def _x_spec():
    return pl.BlockSpec((None, TM, D), lambda b, i: (b, i, 0))


def _mod_spec(n_ctx_tiles):
    return pl.BlockSpec((None, None, 6, D), lambda b, i: (b, jnp.where(i >= n_ctx_tiles, 1, 0), 0, 0))


def _const_spec(shape):
    nd = len(shape)
    return pl.BlockSpec(shape, lambda b, i: (0,) * nd)


def _ada_kernel(c_ref, w_ref, b_ref, o_ref):
    c = c_ref[...]
    s = c * jax.nn.sigmoid(c)
    o_ref[...] = jnp.dot(s, w_ref[...], preferred_element_type=F32, precision=HIGHEST) + b_ref[...]


def _ada(cond8, w, b):
    nb = 4
    bn = w.shape[1] // nb
    return pl.pallas_call(
        _ada_kernel,
        out_shape=_sds((8, w.shape[1]), F32),
        grid=(nb,),
        in_specs=[pl.BlockSpec((8, D), lambda j: (0, 0)),
                  pl.BlockSpec((D, bn), lambda j: (0, j)),
                  pl.BlockSpec((1, bn), lambda j: (0, j))],
        out_specs=pl.BlockSpec((8, bn), lambda j: (0, j)),
        compiler_params=_params(("parallel",), 40),
        name="ada",
    )(cond8, w, b.reshape(1, -1))


def _conv_in_kernel(x_ref, mod_ref, g_ref, w_ref, b_out, z_out):
    h = _modulate(x_ref[...], g_ref[...], mod_ref[0:1, :], mod_ref[1:2, :]).astype(BF16)
    b_out[...] = jnp.dot(h, w_ref[:, 0:D], preferred_element_type=F32).astype(BF16)
    c_gate = jnp.dot(h, w_ref[:, D:2 * D], preferred_element_type=F32)
    v = jnp.dot(h, w_ref[:, 2 * D:3 * D], preferred_element_type=F32)
    z_out[...] = c_gate * v


def _conv_out_kernel(z_ref, zp_ref, zn_ref, b_ref, cw_ref, w_ref, x_ref, mod_ref, o_ref, *, n_ctx, n_tot):
    i = pl.program_id(1)
    z = z_ref[...]
    row = lax.broadcasted_iota(jnp.int32, (TM, 1), 0)
    grow = row + i * TM
    prev_row = zp_ref[7:8, :]
    next_row = zn_ref[0:1, :]
    zm1 = jnp.where(row == 0, prev_row, pltpu.roll(z, 1, axis=0))
    zp1 = jnp.where(row == TM - 1, next_row, pltpu.roll(z, TM - 1, axis=0))
    zm1 = jnp.where((grow == 0) | (grow == n_ctx), 0.0, zm1)
    zp1 = jnp.where((grow == n_ctx - 1) | (grow == n_tot - 1), 0.0, zp1)
    y = cw_ref[0:1, :] * zm1 + cw_ref[1:2, :] * z + cw_ref[2:3, :] * zp1
    a = (b_ref[...].astype(F32) * y).astype(BF16)
    o_ref[...] = x_ref[...] + mod_ref[2:3, :] * jnp.dot(a, w_ref[...], preferred_element_type=F32)


def _conv_mixer(x, mods, norm_g, w_in, conv_w, w_out, n_ctx):
    B, T, _ = x.shape
    nt = T // TM
    nct = n_ctx // TM
    b_gate, z = pl.pallas_call(
        _conv_in_kernel,
        out_shape=(_sds((B, T, D), BF16), _sds((B, T, D), F32)),
        grid=(B, nt),
        in_specs=[_x_spec(), _mod_spec(nct), _const_spec((1, D)), _const_spec((D, 3 * D))],
        out_specs=(_x_spec(), _x_spec()),
        compiler_params=_params(("parallel", "parallel"), 48),
        name="conv_in",
    )(x, mods, norm_g.reshape(1, D), w_in.astype(BF16))
    r8 = TM // 8
    return pl.pallas_call(
        functools.partial(_conv_out_kernel, n_ctx=n_ctx, n_tot=T),
        out_shape=_sds((B, T, D), F32),
        grid=(B, nt),
        in_specs=[_x_spec(),
                  pl.BlockSpec((None, 8, D), lambda b, i: (b, jnp.maximum(i * r8 - 1, 0), 0)),
                  pl.BlockSpec((None, 8, D), lambda b, i: (b, jnp.minimum((i + 1) * r8, T // 8 - 1), 0)),
                  _x_spec(), _const_spec((3, D)), _const_spec((D, D)), _x_spec(), _mod_spec(nct)],
        out_specs=_x_spec(),
        compiler_params=_params(("parallel", "parallel"), 48),
        name="conv_out",
    )(z, z, z, b_gate, conv_w, w_out.astype(BF16), x, mods)


def _qkv_kernel(x_ref, mod_ref, g_ref, w_ref, cos_ref, slo_ref, shi_ref, qt_out, k_out, vt_out):
    h = _modulate(x_ref[...], g_ref[...], mod_ref[0:1, :], mod_ref[1:2, :]).astype(BF16)
    cos = cos_ref[...]
    s_lo = slo_ref[...]
    s_hi = shi_ref[...]

    def rope(t):
        return t * cos + pltpu.roll(t, 128 - 16, axis=1) * s_lo + pltpu.roll(t, 16, axis=1) * s_hi

    q_scale = (HEAD_DIM ** -0.5) * math.log2(math.e)
    for hd in range(N_HEADS):
        lo, hi = hd * 128, (hd + 1) * 128
        q = jnp.dot(h, w_ref[:, lo:hi], preferred_element_type=F32) * q_scale
        qt_out[hd] = rope(q).T.astype(BF16)
        k = jnp.dot(h, w_ref[:, D + lo:D + hi], preferred_element_type=F32)
        k_out[:, lo:hi] = rope(k).astype(BF16)
        v = jnp.dot(h, w_ref[:, 2 * D + lo:2 * D + hi], preferred_element_type=F32)
        vt_out[hd] = v.astype(BF16).astype(F32).T.astype(BF16)


def _attn_kernel(qt_ref, k_ref, vt_ref, lam_ref, sub_ref, o_ref, s_buf, acc_ref, *,
                 n_ctx_tiles, n_lat_chunks, lam_init):
    i = pl.program_id(2)
    qt = qt_ref[...]
    row = lax.broadcasted_iota(jnp.int32, qt.shape, 0)
    zero = jnp.zeros_like(qt)
    q_cat = jnp.concatenate([jnp.where(row < HEAD_DIM, qt, zero), jnp.where(row >= HEAD_DIM, qt, zero)], axis=1)

    def scores(first_tile, n_tiles):
        start = first_tile * TM
        if not isinstance(first_tile, int):
            start = pl.multiple_of(start, TM)
        return jnp.dot(k_ref[pl.ds(start, n_tiles * TM), :], q_cat, preferred_element_type=F32)

    def absorb(s, first_tile, n_tiles, m, l):
        m_new = jnp.maximum(m, jnp.max(s, axis=0, keepdims=True))
        p = jnp.exp2(s - m_new)
        alpha = jnp.exp2(m - m_new)
        l_new = alpha * l + jnp.sum(p, axis=0, keepdims=True)
        pb = p.astype(BF16)
        pv = jnp.dot(vt_ref[first_tile], pb[0:TM, :], preferred_element_type=F32)
        for c in range(1, n_tiles):
            pv = pv + jnp.dot(vt_ref[first_tile + c], pb[c * TM:(c + 1) * TM, :], preferred_element_type=F32)
        acc_ref[...] = alpha * acc_ref[...] + pv
        return m_new, l_new

    def finish(l):
        acc = acc_ref[...]
        lv = lam_ref[...]
        lam = (jnp.exp(jnp.sum(lv[0:1, :] * lv[1:2, :], axis=-1, keepdims=True))
               - jnp.exp(jnp.sum(lv[2:3, :] * lv[3:4, :], axis=-1, keepdims=True)) + lam_init)
        o = (acc[:, 0:TM] / l[:, 0:TM] - lam * (acc[:, TM:2 * TM] / l[:, TM:2 * TM])).T
        o = o * lax.rsqrt(jnp.mean(o * o, axis=-1, keepdims=True) + EPS)
        o_ref[...] = ((o * sub_ref[...]) * (1.0 - lam_init)).astype(BF16)

    acc_ref[...] = jnp.zeros_like(acc_ref)
    s_ctx = scores(0, n_ctx_tiles)
    s_buf[0] = scores(n_ctx_tiles, 2)
    m_ctx, l_ctx = absorb(s_ctx, 0, n_ctx_tiles, jnp.full((1, 2 * TM), -1e30, F32), jnp.zeros((1, 2 * TM), F32))

    @pl.when(i < n_ctx_tiles)
    def _():
        finish(l_ctx)

    @pl.when(i >= n_ctx_tiles)
    def _():
        last = n_ctx_tiles + 2 * (n_lat_chunks - 1)

        def two_chunks(t, carry):
            m, l = carry
            first = n_ctx_tiles + 4 * t
            s_buf[1] = scores(first + 2, 2)
            m, l = absorb(s_buf[0], first, 2, m, l)
            s_buf[0] = scores(jnp.minimum(first + 4, last), 2)
            return absorb(s_buf[1], first + 2, 2, m, l)

        _, l_all = lax.fori_loop(0, n_lat_chunks // 2, two_chunks, (m_ctx, l_ctx))
        finish(l_all)


def _out_proj_kernel(a_ref, w_ref, x_ref, mod_ref, o_ref):
    o_ref[...] = x_ref[...] + mod_ref[2:3, :] * jnp.dot(a_ref[...], w_ref[...], preferred_element_type=F32)


def _rope_tables(n_ctx, n_lat):
    t = jnp.arange(n_lat, dtype=jnp.int32)
    row = (t // GRID_W).astype(F32)
    col = (t % GRID_W).astype(F32)
    n_freq = HEAD_DIM // 4
    inv = ROPE_BASE ** (-jnp.arange(n_freq, dtype=F32) / n_freq)
    ang_r = row[:, None] * inv
    ang_c = col[:, None] * inv
    zero = jnp.zeros_like(ang_r)
    cos64 = jnp.concatenate([jnp.cos(ang_r), jnp.cos(ang_r), jnp.cos(ang_c), jnp.cos(ang_c)], axis=-1)
    slo64 = jnp.concatenate([-jnp.sin(ang_r), zero, -jnp.sin(ang_c), zero], axis=-1)
    shi64 = jnp.concatenate([zero, jnp.sin(ang_r), zero, jnp.sin(ang_c)], axis=-1)

    def full(tab, ctx_val):
        tab = jnp.concatenate([tab, tab], axis=-1)
        return jnp.concatenate([jnp.full((n_ctx, 128), ctx_val, F32), tab], axis=0)

    return full(cos64, 1.0), full(slo64, 0.0), full(shi64, 0.0)


def _attn_mixer(x, mods, norm_g, w_qkv, lam_vecs, subln, w_o, n_ctx, lam_init):
    B, T, _ = x.shape
    nt = T // TM
    nct = n_ctx // TM
    cos, s_lo, s_hi = _rope_tables(n_ctx, T - n_ctx)
    tab_spec = pl.BlockSpec((TM, 128), lambda b, i: (i, 0))
    assert (T - n_ctx) % (4 * TM) == 0
    t_spec = pl.BlockSpec((None, N_HEADS, None, 128, TM), lambda b, i: (b, 0, i, 0, 0))
    t_shape = _sds((B, N_HEADS, nt, 128, TM), BF16)
    qt, k, vt = pl.pallas_call(
        _qkv_kernel,
        out_shape=(t_shape, _sds((B, T, D), BF16), t_shape),
        grid=(B, nt),
        in_specs=[_x_spec(), _mod_spec(nct), _const_spec((1, D)), _const_spec((D, 3 * D)),
                  tab_spec, tab_spec, tab_spec],
        out_specs=(t_spec, _x_spec(), t_spec),
        compiler_params=_params(("parallel", "parallel"), 48),
        name="qkv",
    )(x, mods, norm_g.reshape(1, D), w_qkv.astype(BF16), cos, s_lo, s_hi)

    attn = pl.pallas_call(
        functools.partial(_attn_kernel, n_ctx_tiles=nct, n_lat_chunks=(T - n_ctx) // (2 * TM), lam_init=lam_init),
        out_shape=_sds((B, T, D), BF16),
        grid=(B, N_HEADS, nt),
        in_specs=[pl.BlockSpec((None, None, None, 128, TM), lambda b, h, i: (b, h, i, 0, 0)),
                  pl.BlockSpec((None, T, 128), lambda b, h, i: (b, 0, h)),
                  pl.BlockSpec((None, None, nt, 128, TM), lambda b, h, i: (b, h, 0, 0, 0)),
                  pl.BlockSpec((4, HEAD_DIM), lambda b, h, i: (0, 0)),
                  pl.BlockSpec((1, V_DIM), lambda b, h, i: (0, 0))],
        out_specs=pl.BlockSpec((None, TM, 128), lambda b, h, i: (b, i, h)),
        scratch_shapes=[pltpu.VMEM((2, 2 * TM, 2 * TM), F32), pltpu.VMEM((V_DIM, 2 * TM), F32)],
        compiler_params=_params(("parallel", "parallel", "parallel"), 48),
        name="diff_attn",
    )(qt, k, vt, lam_vecs, subln.reshape(1, V_DIM))

    return pl.pallas_call(
        _out_proj_kernel,
        out_shape=_sds((B, T, D), F32),
        grid=(B, nt),
        in_specs=[_x_spec(), _const_spec((D, D)), _x_spec(), _mod_spec(nct)],
        out_specs=_x_spec(),
        compiler_params=_params(("parallel", "parallel"), 48),
        name="attn_out",
    )(attn, w_o.astype(BF16), x, mods)


def _mod_only_kernel(x_ref, mod_ref, g_ref, h_out):
    h_out[...] = _modulate(x_ref[...], g_ref[...], mod_ref[0:1, :], mod_ref[1:2, :])


def _s5_chunk_rows(h_ref):
    return jnp.concatenate([h_ref[:, s, :] for s in range(S5_T)], axis=-1).astype(BF16)


def _s5_state_kernel(h_ref, w_ref, sfr, sfi, sbr, sbi):
    s = jnp.dot(_s5_chunk_rows(h_ref), w_ref[...], preferred_element_type=F32)
    sfr[...] = s[:, 0:512]
    sfi[...] = s[:, 512:1024]
    sbr[...] = s[:, 1024:1536]
    sbi[...] = s[:, 1536:2048]


def _s5_scan_kernel(sfr, sfi, sbr, sbi, afr, afi, abr, abi, hfr, hfi, hbr, hbi, *, n_ctx_chunks, n_chunks):
    ar, ai = afr[...], afi[...]

    def fwd(n, carry):
        hr, hi = carry
        hfr[n] = hr
        hfi[n] = hi
        sr, si = sfr[n], sfi[n]
        return ar * hr - ai * hi + sr, ar * hi + ai * hr + si

    zero = jnp.zeros(ar.shape, F32)
    lax.fori_loop(0, n_chunks, fwd, (zero, zero))

    br, bi = abr[...], abi[...]

    def bwd(t, carry, top):
        n = top - t
        hr, hi = carry
        hbr[n] = hr
        hbi[n] = hi
        sr, si = sbr[n], sbi[n]
        return br * hr - bi * hi + sr, br * hi + bi * hr + si

    carry = lax.fori_loop(0, n_ctx_chunks, functools.partial(bwd, top=n_ctx_chunks - 1), (zero, zero))
    lax.fori_loop(0, n_chunks - n_ctx_chunks, functools.partial(bwd, top=n_chunks - 1), carry)


def _s5_out_kernel(h_ref, m_ref, hfr, hfi, hbr, hbi, v_ref, y_ref):
    state = jnp.concatenate([hfr[...], hfi[...], hbr[...], hbi[...]], axis=-1).astype(BF16)
    y = (jnp.dot(_s5_chunk_rows(h_ref), m_ref[...], preferred_element_type=F32)
         + jnp.dot(state, v_ref[...], preferred_element_type=F32))
    for t in range(S5_T // 2):
        y_ref[:, t, :] = y[:, t * 128:(t + 1) * 128]


def _glu_kernel(y_ref, x_ref, mod_ref, g_ref, d_ref, w_ref, o_ref):
    x = x_ref[...]
    hl = _modulate(x, g_ref[...], mod_ref[0:1, :], mod_ref[1:2, :])
    y = y_ref[...] + d_ref[...] * hl
    a = jax.nn.gelu(y).astype(BF16)
    val = jnp.dot(a, w_ref[:, 0:D], preferred_element_type=F32)
    gate = jnp.dot(a, w_ref[:, D:2 * D], preferred_element_type=F32)
    o_ref[...] = x + mod_ref[2:3, :] * (val * jax.nn.sigmoid(gate))


def _s5_matrices(a_re, a_im, log_dt, b_re, b_im, c_re, c_im):
    T = S5_T
    hp = functools.partial(jnp.einsum, precision=HIGHEST)
    pw_re, pw_im, bb_re, bb_im = [], [], [], []
    for d in range(2):
        ar, ai = a_re[d].astype(F32), a_im[d].astype(F32)
        dt = jnp.exp(log_dt[d].astype(F32))[:, None]
        mag = jnp.exp(dt * ar)
        ab_re = mag * jnp.cos(dt * ai)
        ab_im = mag * jnp.sin(dt * ai)
        den = ar * ar + ai * ai
        nr = ab_re - 1.0
        co_re = (nr * ar + ab_im * ai) / den
        co_im = (ab_im * ar - nr * ai) / den
        br, bi = b_re[d].astype(F32), b_im[d].astype(F32)
        bb_re.append(co_re[..., None] * br - co_im[..., None] * bi)
        bb_im.append(co_re[..., None] * bi + co_im[..., None] * br)
        pr = [jnp.ones_like(ab_re)]
        pi = [jnp.zeros_like(ab_re)]
        for _ in range(T):
            pr.append(pr[-1] * ab_re - pi[-1] * ab_im)
            pi.append(pr[-2] * ab_im + pi[-1] * ab_re)
        pw_re.append(jnp.stack(pr))
        pw_im.append(jnp.stack(pi))

    def lag_kernels(d):
        mr = pw_re[d][:T, :, :, None] * bb_re[d][None] - pw_im[d][:T, :, :, None] * bb_im[d][None]
        mi = pw_re[d][:T, :, :, None] * bb_im[d][None] + pw_im[d][:T, :, :, None] * bb_re[d][None]
        return hp('gop,kgpi->kgoi', c_re[d].astype(F32), mr) - hp('gop,kgpi->kgoi', c_im[d].astype(F32), mi)

    kf, kb = lag_kernels(0), lag_kernels(1)
    lag0 = (kf[0] + kb[0])[None]
    kfull = jnp.concatenate([kb[:0:-1], lag0, kf[1:]], axis=0)
    tt = jnp.arange(T)
    idx = tt[None, :] - tt[:, None] + (T - 1)
    mt = kfull[idx]
    mt = mt.transpose(2, 0, 4, 1, 3).reshape(N_GROUPS, T * GROUP_SIZE, T * GROUP_SIZE)

    def end_state(d, powers):
        pr, pi = pw_re[d][powers], pw_im[d][powers]
        wr = pr[..., None] * bb_re[d][None] - pi[..., None] * bb_im[d][None]
        wi = pr[..., None] * bb_im[d][None] + pi[..., None] * bb_re[d][None]
        f = lambda w: w.transpose(1, 0, 3, 2).reshape(N_GROUPS, T * GROUP_SIZE, STATE_DIM)
        return f(wr), f(wi)

    def read_out(d, powers):
        pr, pi = pw_re[d][powers], pw_im[d][powers]
        cr, ci = c_re[d].astype(F32), c_im[d].astype(F32)
        qr = cr[None] * pr[:, :, None, :] - ci[None] * pi[:, :, None, :]
        qi = cr[None] * pi[:, :, None, :] + ci[None] * pr[:, :, None, :]
        f = lambda q: q.transpose(1, 3, 0, 2).reshape(N_GROUPS, STATE_DIM, T * GROUP_SIZE)
        return f(qr), f(-qi)

    wfr, wfi = end_state(0, T - 1 - tt)
    wbr, wbi = end_state(1, tt)
    vfr, vfi = read_out(0, tt + 1)
    vbr, vbi = read_out(1, T - tt)

    gsz = T * GROUP_SIZE
    g8 = jnp.arange(8, dtype=jnp.int32)[:, None]
    k = jnp.arange(gsz, dtype=jnp.int32)[None, :]
    chunk_col = (k // GROUP_SIZE) * 128 + g8 * GROUP_SIZE + k % GROUP_SIZE
    state_col = (k // STATE_DIM) * (8 * STATE_DIM) + g8 * STATE_DIM + k % STATE_DIM
    lanes = jnp.arange(8 * gsz, dtype=jnp.int32)[None, None, :]
    to_chunk = (chunk_col[:, :, None] == lanes).astype(BF16)
    to_state = (state_col[:, :, None] == lanes).astype(BF16)
    w_cat = jnp.stack([wfr, wfi, wbr, wbi], axis=2)
    v_cat = jnp.stack([vfr, vfi, vbr, vbi], axis=1)
    m8 = _spread(mt, to_chunk, T, GROUP_SIZE)
    w8 = _spread(w_cat.reshape(N_GROUPS, gsz, gsz), to_state, T, GROUP_SIZE)
    v8 = _spread(v_cat.reshape(N_GROUPS, gsz, gsz), to_chunk, 4, STATE_DIM)
    a_pow = [t[T].reshape(N_GROUPS // 2, 2 * STATE_DIM) for t in (pw_re[0], pw_im[0], pw_re[1], pw_im[1])]
    return m8, w8, v8, a_pow


def _spread_kernel(a_ref, c_ref, o_ref):
    e = jnp.dot(a_ref[...], c_ref[...], preferred_element_type=F32)
    o_ref[...] = e.reshape(o_ref.shape).astype(BF16)


def _spread(a, col_of, r1, r2):
    nb = N_GROUPS // 8
    kdim, n = col_of.shape[1], col_of.shape[2]
    out = pl.pallas_call(
        _spread_kernel,
        out_shape=_sds((nb, r1, 8, r2, n), BF16),
        grid=(nb, 8),
        in_specs=[pl.BlockSpec((None, None, r1 * r2, kdim), lambda g, j: (g, j, 0, 0)),
                  pl.BlockSpec((None, kdim, n), lambda g, j: (j, 0, 0))],
        out_specs=pl.BlockSpec((None, r1, None, r2, n), lambda g, j: (g, 0, j, 0, 0)),
        compiler_params=_params(("parallel", "parallel"), 32),
        name="s5_spread",
    )(a.astype(BF16).reshape(nb, 8, r1 * r2, kdim), col_of)
    return out.reshape(nb, r1 * 8 * r2, n)


def _s5_mixer(x, mods, norm_g, ssm, n_ctx):
    a_re, a_im, log_dt, b_re, b_im, c_re, c_im, d_skip, w_glu = ssm
    B, T, _ = x.shape
    nt = T // TM
    nct = n_ctx // TM
    nch = T // S5_T
    nb = N_GROUPS // 8
    sw = 8 * STATE_DIM
    cw = S5_T * 128
    m8, w8, v8, a_pow = _s5_matrices(a_re, a_im, log_dt, b_re, b_im, c_re, c_im)

    h = pl.pallas_call(
        _mod_only_kernel,
        out_shape=_sds((B, T, D), F32),
        grid=(B, nt),
        in_specs=[_x_spec(), _mod_spec(nct), _const_spec((1, D))],
        out_specs=_x_spec(),
        compiler_params=_params(("parallel", "parallel"), 32),
        name="s5_mod",
    )(x, mods, norm_g.reshape(1, D))
    h4 = h.reshape(B * nch, S5_T, D)

    rows = B * nch
    state_shape = _sds((rows, nb * sw), F32)
    s_parts = pl.pallas_call(
        _s5_state_kernel,
        out_shape=(state_shape,) * 4,
        grid=(nb, B),
        in_specs=[pl.BlockSpec((nch, S5_T, 128), lambda g, b: (b, 0, g)),
                  pl.BlockSpec((None, cw, 4 * sw), lambda g, b: (g, 0, 0))],
        out_specs=(pl.BlockSpec((nch, sw), lambda g, b: (b, g)),) * 4,
        compiler_params=_params(("parallel", "parallel"), 56),
        name="s5_state",
    )(h4, w8)

    n_pairs = N_GROUPS // 2
    npb = 8
    view = lambda t: t.reshape(B, nch, n_pairs, 128)
    scan_blk = pl.BlockSpec((None, nch, npb, 128), lambda b, j: (b, 0, j, 0))
    coef_blk = pl.BlockSpec((npb, 128), lambda b, j: (j, 0))
    h_parts = pl.pallas_call(
        functools.partial(_s5_scan_kernel, n_ctx_chunks=n_ctx // S5_T, n_chunks=nch),
        out_shape=(_sds((B, nch, n_pairs, 128), F32),) * 4,
        grid=(B, n_pairs // npb),
        in_specs=[scan_blk] * 4 + [coef_blk] * 4,
        out_specs=(scan_blk,) * 4,
        compiler_params=_params(("parallel", "parallel"), 56),
        name="s5_scan",
    )(*[view(t) for t in s_parts], *a_pow)

    half = S5_T // 2
    y4 = pl.pallas_call(
        _s5_out_kernel,
        out_shape=_sds((rows, S5_T, D), F32),
        grid=(nb, 2, B),
        in_specs=[pl.BlockSpec((nch, S5_T, 128), lambda g, hf, b: (b, 0, g)),
                  pl.BlockSpec((None, cw, half * 128), lambda g, hf, b: (g, 0, hf))]
                 + [pl.BlockSpec((nch, sw), lambda g, hf, b: (b, g))] * 4
                 + [pl.BlockSpec((None, 4 * sw, half * 128), lambda g, hf, b: (g, 0, hf))],
        out_specs=pl.BlockSpec((nch, half, 128), lambda g, hf, b: (b, hf, g)),
        compiler_params=_params(("parallel", "parallel", "parallel"), 56),
        name="s5_out",
    )(h4, m8, *[t.reshape(rows, nb * sw) for t in h_parts], v8)
    y = y4.reshape(B, T, D)

    return pl.pallas_call(
        _glu_kernel,
        out_shape=_sds((B, T, D), F32),
        grid=(B, nt),
        in_specs=[_x_spec(), _x_spec(), _mod_spec(nct), _const_spec((1, D)), _const_spec((1, D)),
                  _const_spec((D, 2 * D))],
        out_specs=_x_spec(),
        compiler_params=_params(("parallel", "parallel"), 48),
        name="s5_glu",
    )(y, x, mods, norm_g.reshape(1, D), d_skip.reshape(1, D), w_glu.astype(BF16))


N_FF_CHUNKS = D_FF // FF_CHUNK


def _swiglu_rows(h, wgu_ref, wd_ref, after_chunk=None):
    acc = jnp.zeros((h.shape[0], D), F32)
    for c in range(N_FF_CHUNKS):
        lo, hi = c * FF_CHUNK, (c + 1) * FF_CHUNK
        g = jnp.dot(h, wgu_ref[:, lo:hi], preferred_element_type=F32)
        u = jnp.dot(h, wgu_ref[:, D_FF + lo:D_FF + hi], preferred_element_type=F32)
        a = ((g * jax.nn.sigmoid(g)) * u).astype(BF16)
        acc = acc + jnp.dot(a, wd_ref[lo:hi, :], preferred_element_type=F32)
        if after_chunk is not None:
            after_chunk(c)
    return acc


def _ffn_kernel(x_ref, mod_ref, g_ref, wgu_ref, wd_ref, o_ref):
    x = x_ref[...]
    h = _modulate(x, g_ref[...], mod_ref[3:4, :], mod_ref[4:5, :]).astype(BF16)
    o_ref[...] = x + mod_ref[5:6, :] * _swiglu_rows(h, wgu_ref, wd_ref)


def _dense_ffn(x, mods, norm_g, w_gu, w_down, n_ctx):
    B, T, _ = x.shape
    return pl.pallas_call(
        _ffn_kernel,
        out_shape=_sds((B, T, D), F32),
        grid=(B, T // TM),
        in_specs=[_x_spec(), _mod_spec(n_ctx // TM), _const_spec((1, D)),
                  _const_spec((D, 2 * D_FF)), _const_spec((D_FF, D))],
        out_specs=_x_spec(),
        compiler_params=_params(("parallel", "parallel"), 56),
        name="dense_ffn",
    )(x, mods, norm_g.reshape(1, D), w_gu.astype(BF16), w_down.astype(BF16))


def _router_kernel(x_ref, mod_ref, g_ref, wr_ref, h_out, r_out):
    h = _modulate(x_ref[...], g_ref[...], mod_ref[3:4, :], mod_ref[4:5, :])
    h_out[...] = h
    logits = jnp.dot(h, wr_ref[...], preferred_element_type=F32, precision=HIGHEST)
    lane = lax.broadcasted_iota(jnp.int32, logits.shape, 1)
    neg = jnp.float32(-jnp.inf)
    logits = jnp.where(lane < N_EXPERTS, logits, neg)
    v1 = jnp.max(logits, axis=-1, keepdims=True)
    i1 = jnp.min(jnp.where(logits == v1, lane, 128), axis=-1, keepdims=True)
    rest = jnp.where(lane == i1, neg, logits)
    v2 = jnp.max(rest, axis=-1, keepdims=True)
    i2 = jnp.min(jnp.where(rest == v2, lane, 128), axis=-1, keepdims=True)
    e = jnp.exp(v2 - v1)
    w1 = 1.0 / (1.0 + e)
    w2 = e / (1.0 + e)
    out_lane = lax.broadcasted_iota(jnp.int32, (TM, 8), 1)
    r = jnp.where(out_lane == 0, i1.astype(F32), 0.0)
    r = jnp.where(out_lane == 1, i2.astype(F32), r)
    r = jnp.where(out_lane == 2, w1, r)
    r = jnp.where(out_lane == 3, w2, r)
    r_out[...] = r


ROW_UNROLL = 8


def _gather_rows(idx_of, src_hbm, dst, sem, n_rows, *, wait):
    def body(g, carry):
        for u in range(ROW_UNROLL):
            r = g * ROW_UNROLL + u
            if wait:
                pltpu.make_async_copy(src_hbm.at[pl.ds(0, 1)], dst.at[pl.ds(r, 1)], sem).wait()
            else:
                pltpu.make_async_copy(src_hbm.at[pl.ds(idx_of(r), 1)], dst.at[pl.ds(r, 1)], sem).start(priority=u % 2)
        return carry

    lax.fori_loop(0, n_rows // ROW_UNROLL, body, 0)


def _expert_kernel(meta_ref, src_ref, nxt_ref, h_hbm, wgu_ref, wd_ref, y_ref, hbuf, sems):
    t = pl.program_id(0)
    n_used = meta_ref[pl.num_programs(0)]
    slot = lax.rem(t, 2)
    rows_per_chunk = -(-TME // N_FF_CHUNKS)

    @pl.when(t == 0)
    def _():
        _gather_rows(lambda r: src_ref[0, 0, r], h_hbm, hbuf.at[0], sems.at[0], TME, wait=False)

    @pl.when(t < n_used)
    def _():
        _gather_rows(None, h_hbm, hbuf.at[slot], sems.at[slot], TME, wait=True)

        def start_next_rows(c):
            for r in range(c * rows_per_chunk, min((c + 1) * rows_per_chunk, TME)):
                pltpu.make_async_copy(h_hbm.at[pl.ds(nxt_ref[0, 0, r], 1)], hbuf.at[1 - slot, pl.ds(r, 1)],
                                      sems.at[1 - slot]).start(priority=r % 2)

        y_ref[...] = _swiglu_rows(hbuf[slot].astype(BF16), wgu_ref, wd_ref, start_next_rows)

    @pl.when(t == n_used)
    def _():
        _gather_rows(None, h_hbm, hbuf.at[slot], sems.at[slot], TME, wait=True)

    @pl.when(t >= n_used)
    def _():
        y_ref[...] = jnp.zeros_like(y_ref)


def _combine_kernel(pos_ref, nxt_ref, y_hbm, x_ref, mod_ref, r_ref, o_ref, ybuf, sems):
    t = pl.program_id(0)
    slot = lax.rem(t, 2)

    def fetch(idx_ref, s):
        for k in range(2):
            _gather_rows(lambda r, k=k: idx_ref[0, k, r], y_hbm, ybuf.at[s, k], sems.at[s], TM, wait=False)

    @pl.when(t == 0)
    def _():
        fetch(pos_ref, 0)

    @pl.when(t + 1 < pl.num_programs(0))
    def _():
        fetch(nxt_ref, 1 - slot)

    for k in range(2):
        _gather_rows(None, y_hbm, ybuf.at[slot, k], sems.at[slot], TM, wait=True)
    r = r_ref[...]
    out = r[:, 2:3] * ybuf[slot, 0] + r[:, 3:4] * ybuf[slot, 1]
    o_ref[...] = x_ref[...] + mod_ref[5:6, :] * out


def _final_norm_kernel(x_ref, g_ref, o_ref):
    x = x_ref[...]
    o_ref[...] = (x * lax.rsqrt(jnp.mean(x * x, axis=-1, keepdims=True) + EPS)) * g_ref[...]


def _moe_ffn(x, mods, norm_g, w_router, w_gu, w_down, n_ctx):
    B, T, _ = x.shape
    nt = T // TM
    n_tok = B * T
    wr = jnp.zeros((D, 128), F32).at[:, :N_EXPERTS].set(w_router)
    h, route = pl.pallas_call(
        _router_kernel,
        out_shape=(_sds((B, T, D), F32), _sds((B, T, 8), F32)),
        grid=(B, nt),
        in_specs=[_x_spec(), _mod_spec(n_ctx // TM), _const_spec((1, D)), _const_spec((D, 128))],
        out_specs=(_x_spec(), pl.BlockSpec((None, TM, 8), lambda b, i: (b, i, 0))),
        compiler_params=_params(("parallel", "parallel"), 32),
        name="router",
    )(x, mods, norm_g.reshape(1, D), wr)

    eid = route.reshape(n_tok, 8)[:, :2].astype(jnp.int32).reshape(-1)
    onehot = (eid[:, None] == jnp.arange(N_EXPERTS, dtype=jnp.int32)[None, :]).astype(jnp.int32)
    rank = jnp.sum((jnp.cumsum(onehot, axis=0) - onehot) * onehot, axis=1)
    counts = jnp.sum(onehot, axis=0)
    padded = ((counts + TME - 1) // TME) * TME
    ends = jnp.cumsum(padded)
    starts = ends - padded
    pos = starts[eid] + rank
    n_rows = 2 * n_tok + N_EXPERTS * TME
    n_tiles = n_rows // TME
    src = jnp.zeros((n_rows,), jnp.int32).at[pos].set(jnp.arange(2 * n_tok, dtype=jnp.int32) // 2)
    tile_row = jnp.arange(n_tiles, dtype=jnp.int32)[:, None] * TME
    tile_exp = jnp.minimum(jnp.sum((ends[None, :] <= tile_row).astype(jnp.int32), axis=1), N_EXPERTS - 1)
    meta = jnp.concatenate([tile_exp, (ends[-1:] // TME).astype(jnp.int32)])
    src3 = src.reshape(n_tiles, 1, TME)

    y = pl.pallas_call(
        _expert_kernel,
        out_shape=_sds((n_rows, D), F32),
        grid_spec=pltpu.PrefetchScalarGridSpec(
            num_scalar_prefetch=1,
            grid=(n_tiles,),
            in_specs=[pl.BlockSpec((1, 1, TME), lambda t, meta: (t, 0, 0), memory_space=pltpu.SMEM),
                      pl.BlockSpec((1, 1, TME), lambda t, meta: (jnp.minimum(t + 1, n_tiles - 1), 0, 0),
                                   memory_space=pltpu.SMEM),
                      pl.BlockSpec(memory_space=pl.ANY),
                      pl.BlockSpec((None, D, 2 * D_FF), lambda t, meta: (meta[t], 0, 0)),
                      pl.BlockSpec((None, D_FF, D), lambda t, meta: (meta[t], 0, 0))],
            out_specs=pl.BlockSpec((TME, D), lambda t, meta: (t, 0)),
            scratch_shapes=[pltpu.VMEM((2, TME, D), F32), pltpu.SemaphoreType.DMA((2,))]),
        compiler_params=_params(("arbitrary",), 56),
        name="experts",
    )(meta, src3, src3, h.reshape(n_tok, D), w_gu.astype(BF16), w_down.astype(BF16))

    pos_t = pos.reshape(B * nt, TM, 2).transpose(0, 2, 1)
    n_steps = B * nt
    nct = n_ctx // TM
    tok_spec = lambda last: pl.BlockSpec((None, TM, last), lambda t: (t // nt, t % nt, 0))
    return pl.pallas_call(
        _combine_kernel,
        out_shape=_sds((B, T, D), F32),
        grid=(n_steps,),
        in_specs=[pl.BlockSpec((1, 2, TM), lambda t: (t, 0, 0), memory_space=pltpu.SMEM),
                  pl.BlockSpec((1, 2, TM), lambda t: (jnp.minimum(t + 1, n_steps - 1), 0, 0),
                               memory_space=pltpu.SMEM),
                  pl.BlockSpec(memory_space=pl.ANY),
                  tok_spec(D),
                  pl.BlockSpec((None, None, 6, D), lambda t: (t // nt, jnp.where(t % nt >= nct, 1, 0), 0, 0)),
                  tok_spec(8)],
        out_specs=tok_spec(D),
        scratch_shapes=[pltpu.VMEM((2, 2, TM, D), F32), pltpu.SemaphoreType.DMA((2,))],
        compiler_params=_params(("arbitrary",), 32),
        name="moe_combine",
    )(pos_t, pos_t, y, x, mods, route)


def kernel(x, c, ctx, c_ctx, l0_ada_w, l0_ada_b, l0_norm_mix, l0_norm_ffn, l0_conv_w_in, l0_conv_w, l0_conv_w_out, l0_ffn_w_gu, l0_ffn_w_down, l1_ada_w, l1_ada_b, l1_norm_mix, l1_norm_ffn, l1_attn_w_qkv, l1_attn_lam, l1_attn_subln, l1_attn_w_o, l1_moe_router, l1_moe_w_gu, l1_moe_w_down, l2_ada_w, l2_ada_b, l2_norm_mix, l2_norm_ffn, l2_ssm_a_re, l2_ssm_a_im, l2_ssm_log_dt, l2_ssm_b_re, l2_ssm_b_im, l2_ssm_c_re, l2_ssm_c_im, l2_ssm_d, l2_ssm_w_glu, l2_ffn_w_gu, l2_ffn_w_down, l3_ada_w, l3_ada_b, l3_norm_mix, l3_norm_ffn, l3_conv_w_in, l3_conv_w, l3_conv_w_out, l3_moe_router, l3_moe_w_gu, l3_moe_w_down, final_norm):
    B, n_lat, _ = x.shape
    n_ctx = ctx.shape[1]
    T = n_ctx + n_lat
    assert n_ctx % TM == 0 and n_lat % TM == 0 and n_lat % GRID_W == 0 and B < 8
    xs = jnp.concatenate([ctx, x], axis=1)
    cond8 = jnp.concatenate([c, c_ctx[None], jnp.zeros((8 - B - 1, D), F32)], axis=0)

    def mods_of(w, b):
        m = _ada(cond8, w, b).reshape(8, 6, D)
        return jnp.stack([jnp.broadcast_to(m[B], (B, 6, D)), m[:B]], axis=1)

    mods = mods_of(l0_ada_w, l0_ada_b)
    xs = _conv_mixer(xs, mods, l0_norm_mix, l0_conv_w_in, l0_conv_w, l0_conv_w_out, n_ctx)
    xs = _dense_ffn(xs, mods, l0_norm_ffn, l0_ffn_w_gu, l0_ffn_w_down, n_ctx)

    mods = mods_of(l1_ada_w, l1_ada_b)
    xs = _attn_mixer(xs, mods, l1_norm_mix, l1_attn_w_qkv, l1_attn_lam, l1_attn_subln, l1_attn_w_o, n_ctx,
                     0.8 - 0.6 * math.exp(-0.3 * 1))
    xs = _moe_ffn(xs, mods, l1_norm_ffn, l1_moe_router, l1_moe_w_gu, l1_moe_w_down, n_ctx)

    mods = mods_of(l2_ada_w, l2_ada_b)
    xs = _s5_mixer(xs, mods, l2_norm_mix,
                   (l2_ssm_a_re, l2_ssm_a_im, l2_ssm_log_dt, l2_ssm_b_re, l2_ssm_b_im, l2_ssm_c_re, l2_ssm_c_im,
                    l2_ssm_d, l2_ssm_w_glu), n_ctx)
    xs = _dense_ffn(xs, mods, l2_norm_ffn, l2_ffn_w_gu, l2_ffn_w_down, n_ctx)

    mods = mods_of(l3_ada_w, l3_ada_b)
    xs = _conv_mixer(xs, mods, l3_norm_mix, l3_conv_w_in, l3_conv_w, l3_conv_w_out, n_ctx)
    xs = _moe_ffn(xs, mods, l3_norm_ffn, l3_moe_router, l3_moe_w_gu, l3_moe_w_down, n_ctx)

    nct = n_ctx // TM
    return pl.pallas_call(
        _final_norm_kernel,
        out_shape=_sds((B, n_lat, D), F32),
        grid=(B, n_lat // TM),
        in_specs=[pl.BlockSpec((None, TM, D), lambda b, i: (b, i + nct, 0)), _const_spec((1, D))],
        out_specs=_x_spec(),
        compiler_params=_params(("parallel", "parallel"), 32),
        name="final_norm",
    )(xs, final_norm.reshape(1, D))
```

```python
import functools
import math

import jax
import jax.numpy as jnp
from jax import lax
from jax.experimental import pallas as pl
from jax.experimental.pallas import tpu as pltpu

F32 = jnp.float32
BF16 = jnp.bfloat16
HIGHEST = lax.Precision.HIGHEST

D = 1024
GRID_W = 64
N_HEADS = 8
HEAD_DIM = 64
V_DIM = 128
ROPE_BASE = 10000.0
GROUP_SIZE = 16
N_GROUPS = 64
STATE_DIM = 64
D_FF = 2816
N_EXPERTS = 8
EPS = 1e-6

TM = 256
FF_CHUNK = 256
TK = 256
S5_T = 16
TME = 256
MIB = 1 << 20


def _sds(shape, dtype):
    return jax.ShapeDtypeStruct(shape, dtype)


def _params(sem, vmem_mib):
    return pltpu.CompilerParams(dimension_semantics=sem, vmem_limit_bytes=vmem_mib * MIB)


def _modulate(x, g, shift, scale):
    y = x * lax.rsqrt(jnp.mean(x * x, axis=-1, keepdims=True) + EPS)
    return (y * g) * (1.0 + scale) + shift


def _x_spec():
    return pl.BlockSpec((None, TM, D), lambda b, i: (b, i, 0))


def _mod_spec(n_ctx_tiles):
    return pl.BlockSpec((None, None, 6, D), lambda b, i: (b, jnp.where(i >= n_ctx_tiles, 1, 0), 0, 0))


def _const_spec(shape):
    nd = len(shape)
    return pl.BlockSpec(shape, lambda b, i: (0,) * nd)


def _ada_kernel(c_ref, w_ref, b_ref, o_ref):
    c = c_ref[...]
    s = c * jax.nn.sigmoid(c)
    o_ref[...] = jnp.dot(s, w_ref[...], preferred_element_type=F32, precision=HIGHEST) + b_ref[...]


def _ada(cond8, w, b):
    nb = 4
    bn = w.shape[1] // nb
    return pl.pallas_call(
        _ada_kernel,
        out_shape=_sds((8, w.shape[1]), F32),
        grid=(nb,),
        in_specs=[pl.BlockSpec((8, D), lambda j: (0, 0)),
                  pl.BlockSpec((D, bn), lambda j: (0, j)),
                  pl.BlockSpec((1, bn), lambda j: (0, j))],
        out_specs=pl.BlockSpec((8, bn), lambda j: (0, j)),
        compiler_params=_params(("parallel",), 40),
        name="ada",
    )(cond8, w, b.reshape(1, -1))


def _conv_in_kernel(x_ref, mod_ref, g_ref, w_ref, b_out, z_out):
    h = _modulate(x_ref[...], g_ref[...], mod_ref[0:1, :], mod_ref[1:2, :]).astype(BF16)
    b_out[...] = jnp.dot(h, w_ref[:, 0:D], preferred_element_type=F32).astype(BF16)
    c_gate = jnp.dot(h, w_ref[:, D:2 * D], preferred_element_type=F32)
    v = jnp.dot(h, w_ref[:, 2 * D:3 * D], preferred_element_type=F32)
    z_out[...] = c_gate * v


def _conv_out_kernel(z_ref, zp_ref, zn_ref, b_ref, cw_ref, w_ref, x_ref, mod_ref, o_ref, *, n_ctx, n_tot):
    i = pl.program_id(1)
    z = z_ref[...]
    row = lax.broadcasted_iota(jnp.int32, (TM, 1), 0)
    grow = row + i * TM
    prev_row = zp_ref[7:8, :]
    next_row = zn_ref[0:1, :]
    zm1 = jnp.where(row == 0, prev_row, pltpu.roll(z, 1, axis=0))
    zp1 = jnp.where(row == TM - 1, next_row, pltpu.roll(z, TM - 1, axis=0))
    zm1 = jnp.where((grow == 0) | (grow == n_ctx), 0.0, zm1)
    zp1 = jnp.where((grow == n_ctx - 1) | (grow == n_tot - 1), 0.0, zp1)
    y = cw_ref[0:1, :] * zm1 + cw_ref[1:2, :] * z + cw_ref[2:3, :] * zp1
    a = (b_ref[...].astype(F32) * y).astype(BF16)
    o_ref[...] = x_ref[...] + mod_ref[2:3, :] * jnp.dot(a, w_ref[...], preferred_element_type=F32)


def _conv_mixer(x, mods, norm_g, w_in, conv_w, w_out, n_ctx):
    B, T, _ = x.shape
    nt = T // TM
    nct = n_ctx // TM
    b_gate, z = pl.pallas_call(
        _conv_in_kernel,
        out_shape=(_sds((B, T, D), BF16), _sds((B, T, D), F32)),
        grid=(B, nt),
        in_specs=[_x_spec(), _mod_spec(nct), _const_spec((1, D)), _const_spec((D, 3 * D))],
        out_specs=(_x_spec(), _x_spec()),
        compiler_params=_params(("parallel", "parallel"), 48),
        name="conv_in",
    )(x, mods, norm_g.reshape(1, D), w_in.astype(BF16))
    r8 = TM // 8
    return pl.pallas_call(
        functools.partial(_conv_out_kernel, n_ctx=n_ctx, n_tot=T),
        out_shape=_sds((B, T, D), F32),
        grid=(B, nt),
        in_specs=[_x_spec(),
                  pl.BlockSpec((None, 8, D), lambda b, i: (b, jnp.maximum(i * r8 - 1, 0), 0)),
                  pl.BlockSpec((None, 8, D), lambda b, i: (b, jnp.minimum((i + 1) * r8, T // 8 - 1), 0)),
                  _x_spec(), _const_spec((3, D)), _const_spec((D, D)), _x_spec(), _mod_spec(nct)],
        out_specs=_x_spec(),
        compiler_params=_params(("parallel", "parallel"), 48),
        name="conv_out",
    )(z, z, z, b_gate, conv_w, w_out.astype(BF16), x, mods)


def _qkv_kernel(x_ref, mod_ref, g_ref, w_ref, cos_ref, slo_ref, shi_ref, qt_out, k_out, vt_out):
    h = _modulate(x_ref[...], g_ref[...], mod_ref[0:1, :], mod_ref[1:2, :]).astype(BF16)
    cos = cos_ref[...]
    s_lo = slo_ref[...]
    s_hi = shi_ref[...]

    def rope(t):
        return t * cos + pltpu.roll(t, 128 - 16, axis=1) * s_lo + pltpu.roll(t, 16, axis=1) * s_hi

    q_scale = (HEAD_DIM ** -0.5) * math.log2(math.e)
    for hd in range(N_HEADS):
        lo, hi = hd * 128, (hd + 1) * 128
        q = jnp.dot(h, w_ref[:, lo:hi], preferred_element_type=F32) * q_scale
        qt_out[hd] = rope(q).T.astype(BF16)
        k = jnp.dot(h, w_ref[:, D + lo:D + hi], preferred_element_type=F32)
        k_out[:, lo:hi] = rope(k).astype(BF16)
        v = jnp.dot(h, w_ref[:, 2 * D + lo:2 * D + hi], preferred_element_type=F32)
        vt_out[hd] = v.astype(BF16).astype(F32).T.astype(BF16)


def _attn_kernel(qt_ref, k_ref, vt_ref, lam_ref, sub_ref, o_ref, s_buf, acc_ref, *,
                 n_ctx_tiles, n_lat_chunks, lam_init):
    i = pl.program_id(2)
    qt = qt_ref[...]
    row = lax.broadcasted_iota(jnp.int32, qt.shape, 0)
    zero = jnp.zeros_like(qt)
    q_cat = jnp.concatenate([jnp.where(row < HEAD_DIM, qt, zero), jnp.where(row >= HEAD_DIM, qt, zero)], axis=1)

    def scores(first_tile, n_tiles):
        start = first_tile * TM
        if not isinstance(first_tile, int):
            start = pl.multiple_of(start, TM)
        return jnp.dot(k_ref[pl.ds(start, n_tiles * TM), :], q_cat, preferred_element_type=F32)

    def absorb(s, first_tile, n_tiles, m, l):
        m_new = jnp.maximum(m, jnp.max(s, axis=0, keepdims=True))
        p = jnp.exp2(s - m_new)
        alpha = jnp.exp2(m - m_new)
        l_new = alpha * l + jnp.sum(p, axis=0, keepdims=True)
        pb = p.astype(BF16)
        pv = jnp.dot(vt_ref[first_tile], pb[0:TM, :], preferred_element_type=F32)
        for c in range(1, n_tiles):
            pv = pv + jnp.dot(vt_ref[first_tile + c], pb[c * TM:(c + 1) * TM, :], preferred_element_type=F32)
        acc_ref[...] = alpha * acc_ref[...] + pv
        return m_new, l_new

    def finish(l):
        acc = acc_ref[...]
        lv = lam_ref[...]
        lam = (jnp.exp(jnp.sum(lv[0:1, :] * lv[1:2, :], axis=-1, keepdims=True))
               - jnp.exp(jnp.sum(lv[2:3, :] * lv[3:4, :], axis=-1, keepdims=True)) + lam_init)
        o = (acc[:, 0:TM] / l[:, 0:TM] - lam * (acc[:, TM:2 * TM] / l[:, TM:2 * TM])).T
        o = o * lax.rsqrt(jnp.mean(o * o, axis=-1, keepdims=True) + EPS)
        o_ref[...] = ((o * sub_ref[...]) * (1.0 - lam_init)).astype(BF16)

    acc_ref[...] = jnp.zeros_like(acc_ref)
    s_ctx = scores(0, n_ctx_tiles)
    s_buf[0] = scores(n_ctx_tiles, 2)
    m_ctx, l_ctx = absorb(s_ctx, 0, n_ctx_tiles, jnp.full((1, 2 * TM), -1e30, F32), jnp.zeros((1, 2 * TM), F32))

    @pl.when(i < n_ctx_tiles)
    def _():
        finish(l_ctx)

    @pl.when(i >= n_ctx_tiles)
    def _():
        last = n_ctx_tiles + 2 * (n_lat_chunks - 1)

        def two_chunks(t, carry):
            m, l = carry
            first = n_ctx_tiles + 4 * t
            s_buf[1] = scores(first + 2, 2)
            m, l = absorb(s_buf[0], first, 2, m, l)
            s_buf[0] = scores(jnp.minimum(first + 4, last), 2)
            return absorb(s_buf[1], first + 2, 2, m, l)

        _, l_all = lax.fori_loop(0, n_lat_chunks // 2, two_chunks, (m_ctx, l_ctx))
        finish(l_all)


def _out_proj_kernel(a_ref, w_ref, x_ref, mod_ref, o_ref):
    o_ref[...] = x_ref[...] + mod_ref[2:3, :] * jnp.dot(a_ref[...], w_ref[...], preferred_element_type=F32)


def _rope_tables(n_ctx, n_lat):
    t = jnp.arange(n_lat, dtype=jnp.int32)
    row = (t // GRID_W).astype(F32)
    col = (t % GRID_W).astype(F32)
    n_freq = HEAD_DIM // 4
    inv = ROPE_BASE ** (-jnp.arange(n_freq, dtype=F32) / n_freq)
    ang_r = row[:, None] * inv
    ang_c = col[:, None] * inv
    zero = jnp.zeros_like(ang_r)
    cos64 = jnp.concatenate([jnp.cos(ang_r), jnp.cos(ang_r), jnp.cos(ang_c), jnp.cos(ang_c)], axis=-1)
    slo64 = jnp.concatenate([-jnp.sin(ang_r), zero, -jnp.sin(ang_c), zero], axis=-1)
    shi64 = jnp.concatenate([zero, jnp.sin(ang_r), zero, jnp.sin(ang_c)], axis=-1)

    def full(tab, ctx_val):
        tab = jnp.concatenate([tab, tab], axis=-1)
        return jnp.concatenate([jnp.full((n_ctx, 128), ctx_val, F32), tab], axis=0)

    return full(cos64, 1.0), full(slo64, 0.0), full(shi64, 0.0)


def _attn_mixer(x, mods, norm_g, w_qkv, lam_vecs, subln, w_o, n_ctx, lam_init):
    B, T, _ = x.shape
    nt = T // TM
    nct = n_ctx // TM
    cos, s_lo, s_hi = _rope_tables(n_ctx, T - n_ctx)
    tab_spec = pl.BlockSpec((TM, 128), lambda b, i: (i, 0))
    assert (T - n_ctx) % (4 * TM) == 0
    t_spec = pl.BlockSpec((None, N_HEADS, None, 128, TM), lambda b, i: (b, 0, i, 0, 0))
    t_shape = _sds((B, N_HEADS, nt, 128, TM), BF16)
    qt, k, vt = pl.pallas_call(
        _qkv_kernel,
        out_shape=(t_shape, _sds((B, T, D), BF16), t_shape),
        grid=(B, nt),
        in_specs=[_x_spec(), _mod_spec(nct), _const_spec((1, D)), _const_spec((D, 3 * D)),
                  tab_spec, tab_spec, tab_spec],
        out_specs=(t_spec, _x_spec(), t_spec),
        compiler_params=_params(("parallel", "parallel"), 48),
        name="qkv",
    )(x, mods, norm_g.reshape(1, D), w_qkv.astype(BF16), cos, s_lo, s_hi)

    attn = pl.pallas_call(
        functools.partial(_attn_kernel, n_ctx_tiles=nct, n_lat_chunks=(T - n_ctx) // (2 * TM), lam_init=lam_init),
        out_shape=_sds((B, T, D), BF16),
        grid=(B, N_HEADS, nt),
        in_specs=[pl.BlockSpec((None, None, None, 128, TM), lambda b, h, i: (b, h, i, 0, 0)),
                  pl.BlockSpec((None, T, 128), lambda b, h, i: (b, 0, h)),
                  pl.BlockSpec((None, None, nt, 128, TM), lambda b, h, i: (b, h, 0, 0, 0)),
                  pl.BlockSpec((4, HEAD_DIM), lambda b, h, i: (0, 0)),
                  pl.BlockSpec((1, V_DIM), lambda b, h, i: (0, 0))],
        out_specs=pl.BlockSpec((None, TM, 128), lambda b, h, i: (b, i, h)),
        scratch_shapes=[pltpu.VMEM((2, 2 * TM, 2 * TM), F32), pltpu.VMEM((V_DIM, 2 * TM), F32)],
        compiler_params=_params(("parallel", "parallel", "parallel"), 48),
        name="diff_attn",
    )(qt, k, vt, lam_vecs, subln.reshape(1, V_DIM))

    return pl.pallas_call(
        _out_proj_kernel,
        out_shape=_sds((B, T, D), F32),
        grid=(B, nt),
        in_specs=[_x_spec(), _const_spec((D, D)), _x_spec(), _mod_spec(nct)],
        out_specs=_x_spec(),
        compiler_params=_params(("parallel", "parallel"), 48),
        name="attn_out",
    )(attn, w_o.astype(BF16), x, mods)


def _mod_only_kernel(x_ref, mod_ref, g_ref, h_out):
    h_out[...] = _modulate(x_ref[...], g_ref[...], mod_ref[0:1, :], mod_ref[1:2, :])


def _s5_chunk_rows(h_ref):
    return jnp.concatenate([h_ref[:, s, :] for s in range(S5_T)], axis=-1).astype(BF16)


def _s5_state_kernel(h_ref, w_ref, sfr, sfi, sbr, sbi):
    s = jnp.dot(_s5_chunk_rows(h_ref), w_ref[...], preferred_element_type=F32)
    sfr[...] = s[:, 0:512]
    sfi[...] = s[:, 512:1024]
    sbr[...] = s[:, 1024:1536]
    sbi[...] = s[:, 1536:2048]


def _s5_scan_kernel(sfr, sfi, sbr, sbi, afr, afi, abr, abi, hfr, hfi, hbr, hbi, *, n_ctx_chunks, n_chunks):
    ar, ai = afr[...], afi[...]

    def fwd(n, carry):
        hr, hi = carry
        hfr[n] = hr
        hfi[n] = hi
        sr, si = sfr[n], sfi[n]
        return ar * hr - ai * hi + sr, ar * hi + ai * hr + si

    zero = jnp.zeros(ar.shape, F32)
    lax.fori_loop(0, n_chunks, fwd, (zero, zero))

    br, bi = abr[...], abi[...]

    def bwd(t, carry, top):
        n = top - t
        hr, hi = carry
        hbr[n] = hr
        hbi[n] = hi
        sr, si = sbr[n], sbi[n]
        return br * hr - bi * hi + sr, br * hi + bi * hr + si

    carry = lax.fori_loop(0, n_ctx_chunks, functools.partial(bwd, top=n_ctx_chunks - 1), (zero, zero))
    lax.fori_loop(0, n_chunks - n_ctx_chunks, functools.partial(bwd, top=n_chunks - 1), carry)


def _s5_out_kernel(h_ref, m_ref, hfr, hfi, hbr, hbi, v_ref, y_ref):
    state = jnp.concatenate([hfr[...], hfi[...], hbr[...], hbi[...]], axis=-1).astype(BF16)
    y = (jnp.dot(_s5_chunk_rows(h_ref), m_ref[...], preferred_element_type=F32)
         + jnp.dot(state, v_ref[...], preferred_element_type=F32))
    for t in range(S5_T // 2):
        y_ref[:, t, :] = y[:, t * 128:(t + 1) * 128]


def _glu_kernel(y_ref, x_ref, mod_ref, g_ref, d_ref, w_ref, o_ref):
    x = x_ref[...]
    hl = _modulate(x, g_ref[...], mod_ref[0:1, :], mod_ref[1:2, :])
    y = y_ref[...] + d_ref[...] * hl
    a = jax.nn.gelu(y).astype(BF16)
    val = jnp.dot(a, w_ref[:, 0:D], preferred_element_type=F32)
    gate = jnp.dot(a, w_ref[:, D:2 * D], preferred_element_type=F32)
    o_ref[...] = x + mod_ref[2:3, :] * (val * jax.nn.sigmoid(gate))


def _s5_matrices(a_re, a_im, log_dt, b_re, b_im, c_re, c_im):
    T = S5_T
    hp = functools.partial(jnp.einsum, precision=HIGHEST)
    pw_re, pw_im, bb_re, bb_im = [], [], [], []
    for d in range(2):
        ar, ai = a_re[d].astype(F32), a_im[d].astype(F32)
        dt = jnp.exp(log_dt[d].astype(F32))[:, None]
        mag = jnp.exp(dt * ar)
        ab_re = mag * jnp.cos(dt * ai)
        ab_im = mag * jnp.sin(dt * ai)
        den = ar * ar + ai * ai
        nr = ab_re - 1.0
        co_re = (nr * ar + ab_im * ai) / den
        co_im = (ab_im * ar - nr * ai) / den
        br, bi = b_re[d].astype(F32), b_im[d].astype(F32)
        bb_re.append(co_re[..., None] * br - co_im[..., None] * bi)
        bb_im.append(co_re[..., None] * bi + co_im[..., None] * br)
        pr = [jnp.ones_like(ab_re)]
        pi = [jnp.zeros_like(ab_re)]
        for _ in range(T):
            pr.append(pr[-1] * ab_re - pi[-1] * ab_im)
            pi.append(pr[-2] * ab_im + pi[-1] * ab_re)
        pw_re.append(jnp.stack(pr))
        pw_im.append(jnp.stack(pi))

    def lag_kernels(d):
        mr = pw_re[d][:T, :, :, None] * bb_re[d][None] - pw_im[d][:T, :, :, None] * bb_im[d][None]
        mi = pw_re[d][:T, :, :, None] * bb_im[d][None] + pw_im[d][:T, :, :, None] * bb_re[d][None]
        return hp('gop,kgpi->kgoi', c_re[d].astype(F32), mr) - hp('gop,kgpi->kgoi', c_im[d].astype(F32), mi)

    kf, kb = lag_kernels(0), lag_kernels(1)
    lag0 = (kf[0] + kb[0])[None]
    kfull = jnp.concatenate([kb[:0:-1], lag0, kf[1:]], axis=0)
    tt = jnp.arange(T)
    idx = tt[None, :] - tt[:, None] + (T - 1)
    mt = kfull[idx]
    mt = mt.transpose(2, 0, 4, 1, 3).reshape(N_GROUPS, T * GROUP_SIZE, T * GROUP_SIZE)

    def end_state(d, powers):
        pr, pi = pw_re[d][powers], pw_im[d][powers]
        wr = pr[..., None] * bb_re[d][None] - pi[..., None] * bb_im[d][None]
        wi = pr[..., None] * bb_im[d][None] + pi[..., None] * bb_re[d][None]
        f = lambda w: w.transpose(1, 0, 3, 2).reshape(N_GROUPS, T * GROUP_SIZE, STATE_DIM)
        return f(wr), f(wi)

    def read_out(d, powers):
        pr, pi = pw_re[d][powers], pw_im[d][powers]
        cr, ci = c_re[d].astype(F32), c_im[d].astype(F32)
        qr = cr[None] * pr[:, :, None, :] - ci[None] * pi[:, :, None, :]
        qi = cr[None] * pi[:, :, None, :] + ci[None] * pr[:, :, None, :]
        f = lambda q: q.transpose(1, 3, 0, 2).reshape(N_GROUPS, STATE_DIM, T * GROUP_SIZE)
        return f(qr), f(-qi)

    wfr, wfi = end_state(0, T - 1 - tt)
    wbr, wbi = end_state(1, tt)
    vfr, vfi = read_out(0, tt + 1)
    vbr, vbi = read_out(1, T - tt)

    gsz = T * GROUP_SIZE
    g8 = jnp.arange(8, dtype=jnp.int32)[:, None]
    k = jnp.arange(gsz, dtype=jnp.int32)[None, :]
    chunk_col = (k // GROUP_SIZE) * 128 + g8 * GROUP_SIZE + k % GROUP_SIZE
    state_col = (k // STATE_DIM) * (8 * STATE_DIM) + g8 * STATE_DIM + k % STATE_DIM
    lanes = jnp.arange(8 * gsz, dtype=jnp.int32)[None, None, :]
    to_chunk = (chunk_col[:, :, None] == lanes).astype(BF16)
    to_state = (state_col[:, :, None] == lanes).astype(BF16)
    w_cat = jnp.stack([wfr, wfi, wbr, wbi], axis=2)
    v_cat = jnp.stack([vfr, vfi, vbr, vbi], axis=1)
    m8 = _spread(mt, to_chunk, T, GROUP_SIZE)
    w8 = _spread(w_cat.reshape(N_GROUPS, gsz, gsz), to_state, T, GROUP_SIZE)
    v8 = _spread(v_cat.reshape(N_GROUPS, gsz, gsz), to_chunk, 4, STATE_DIM)
    a_pow = [t[T].reshape(N_GROUPS // 2, 2 * STATE_DIM) for t in (pw_re[0], pw_im[0], pw_re[1], pw_im[1])]
    return m8, w8, v8, a_pow


def _spread_kernel(a_ref, c_ref, o_ref):
    e = jnp.dot(a_ref[...], c_ref[...], preferred_element_type=F32)
    o_ref[...] = e.reshape(o_ref.shape).astype(BF16)


def _spread(a, col_of, r1, r2):
    nb = N_GROUPS // 8
    kdim, n = col_of.shape[1], col_of.shape[2]
    out = pl.pallas_call(
        _spread_kernel,
        out_shape=_sds((nb, r1, 8, r2, n), BF16),
        grid=(nb, 8),
        in_specs=[pl.BlockSpec((None, None, r1 * r2, kdim), lambda g, j: (g, j, 0, 0)),
                  pl.BlockSpec((None, kdim, n), lambda g, j: (j, 0, 0))],
        out_specs=pl.BlockSpec((None, r1, None, r2, n), lambda g, j: (g, 0, j, 0, 0)),
        compiler_params=_params(("parallel", "parallel"), 32),
        name="s5_spread",
    )(a.astype(BF16).reshape(nb, 8, r1 * r2, kdim), col_of)
    return out.reshape(nb, r1 * 8 * r2, n)


def _s5_mixer(x, mods, norm_g, ssm, n_ctx):
    a_re, a_im, log_dt, b_re, b_im, c_re, c_im, d_skip, w_glu = ssm
    B, T, _ = x.shape
    nt = T // TM
    nct = n_ctx // TM
    nch = T // S5_T
    nb = N_GROUPS // 8
    sw = 8 * STATE_DIM
    cw = S5_T * 128
    m8, w8, v8, a_pow = _s5_matrices(a_re, a_im, log_dt, b_re, b_im, c_re, c_im)

    h = pl.pallas_call(
        _mod_only_kernel,
        out_shape=_sds((B, T, D), F32),
        grid=(B, nt),
        in_specs=[_x_spec(), _mod_spec(nct), _const_spec((1, D))],
        out_specs=_x_spec(),
        compiler_params=_params(("parallel", "parallel"), 32),
        name="s5_mod",
    )(x, mods, norm_g.reshape(1, D))
    h4 = h.reshape(B * nch, S5_T, D)

    rows = B * nch
    state_shape = _sds((rows, nb * sw), F32)
    s_parts = pl.pallas_call(
        _s5_state_kernel,
        out_shape=(state_shape,) * 4,
        grid=(nb, B),
        in_specs=[pl.BlockSpec((nch, S5_T, 128), lambda g, b: (b, 0, g)),
                  pl.BlockSpec((None, cw, 4 * sw), lambda g, b: (g, 0, 0))],
        out_specs=(pl.BlockSpec((nch, sw), lambda g, b: (b, g)),) * 4,
        compiler_params=_params(("parallel", "parallel"), 56),
        name="s5_state",
    )(h4, w8)

    n_pairs = N_GROUPS // 2
    npb = 8
    view = lambda t: t.reshape(B, nch, n_pairs, 128)
    scan_blk = pl.BlockSpec((None, nch, npb, 128), lambda b, j: (b, 0, j, 0))
    coef_blk = pl.BlockSpec((npb, 128), lambda b, j: (j, 0))
    h_parts = pl.pallas_call(
        functools.partial(_s5_scan_kernel, n_ctx_chunks=n_ctx // S5_T, n_chunks=nch),
        out_shape=(_sds((B, nch, n_pairs, 128), F32),) * 4,
        grid=(B, n_pairs // npb),
        in_specs=[scan_blk] * 4 + [coef_blk] * 4,
        out_specs=(scan_blk,) * 4,
        compiler_params=_params(("parallel", "parallel"), 56),
        name="s5_scan",
    )(*[view(t) for t in s_parts], *a_pow)

    half = S5_T // 2
    y4 = pl.pallas_call(
        _s5_out_kernel,
        out_shape=_sds((rows, S5_T, D), F32),
        grid=(nb, 2, B),
        in_specs=[pl.BlockSpec((nch, S5_T, 128), lambda g, hf, b: (b, 0, g)),
                  pl.BlockSpec((None, cw, half * 128), lambda g, hf, b: (g, 0, hf))]
                 + [pl.BlockSpec((nch, sw), lambda g, hf, b: (b, g))] * 4
                 + [pl.BlockSpec((None, 4 * sw, half * 128), lambda g, hf, b: (g, 0, hf))],
        out_specs=pl.BlockSpec((nch, half, 128), lambda g, hf, b: (b, hf, g)),
        compiler_params=_params(("parallel", "parallel", "parallel"), 56),
        name="s5_out",
    )(h4, m8, *[t.reshape(rows, nb * sw) for t in h_parts], v8)
    y = y4.reshape(B, T, D)

    return pl.pallas_call(
        _glu_kernel,
        out_shape=_sds((B, T, D), F32),
        grid=(B, nt),
        in_specs=[_x_spec(), _x_spec(), _mod_spec(nct), _const_spec((1, D)), _const_spec((1, D)),
                  _const_spec((D, 2 * D))],
        out_specs=_x_spec(),
        compiler_params=_params(("parallel", "parallel"), 48),
        name="s5_glu",
    )(y, x, mods, norm_g.reshape(1, D), d_skip.reshape(1, D), w_glu.astype(BF16))


N_FF_CHUNKS = D_FF // FF_CHUNK


def _swiglu_rows(h, wgu_ref, wd_ref, after_chunk=None):
    acc = jnp.zeros((h.shape[0], D), F32)
    for c in range(N_FF_CHUNKS):
        lo, hi = c * FF_CHUNK, (c + 1) * FF_CHUNK
        g = jnp.dot(h, wgu_ref[:, lo:hi], preferred_element_type=F32)
        u = jnp.dot(h, wgu_ref[:, D_FF + lo:D_FF + hi], preferred_element_type=F32)
        a = ((g * jax.nn.sigmoid(g)) * u).astype(BF16)
        acc = acc + jnp.dot(a, wd_ref[lo:hi, :], preferred_element_type=F32)
        if after_chunk is not None:
            after_chunk(c)
    return acc


def _ffn_kernel(x_ref, mod_ref, g_ref, wgu_ref, wd_ref, o_ref):
    x = x_ref[...]
    h = _modulate(x, g_ref[...], mod_ref[3:4, :], mod_ref[4:5, :]).astype(BF16)
    o_ref[...] = x + mod_ref[5:6, :] * _swiglu_rows(h, wgu_ref, wd_ref)


def _dense_ffn(x, mods, norm_g, w_gu, w_down, n_ctx):
    B, T, _ = x.shape
    return pl.pallas_call(
        _ffn_kernel,
        out_shape=_sds((B, T, D), F32),
        grid=(B, T // TM),
        in_specs=[_x_spec(), _mod_spec(n_ctx // TM), _const_spec((1, D)),
                  _const_spec((D, 2 * D_FF)), _const_spec((D_FF, D))],
        out_specs=_x_spec(),
        compiler_params=_params(("parallel", "parallel"), 56),
        name="dense_ffn",
    )(x, mods, norm_g.reshape(1, D), w_gu.astype(BF16), w_down.astype(BF16))


def _router_kernel(x_ref, mod_ref, g_ref, wr_ref, r_out):
    h = _modulate(x_ref[...], g_ref[...], mod_ref[3:4, :], mod_ref[4:5, :])
    logits = jnp.dot(h, wr_ref[...], preferred_element_type=F32, precision=HIGHEST)
    lane = lax.broadcasted_iota(jnp.int32, logits.shape, 1)
    neg = jnp.float32(-jnp.inf)
    logits = jnp.where(lane < N_EXPERTS, logits, neg)
    v1 = jnp.max(logits, axis=-1, keepdims=True)
    i1 = jnp.min(jnp.where(logits == v1, lane, 128), axis=-1, keepdims=True)
    rest = jnp.where(lane == i1, neg, logits)
    v2 = jnp.max(rest, axis=-1, keepdims=True)
    i2 = jnp.min(jnp.where(rest == v2, lane, 128), axis=-1, keepdims=True)
    e = jnp.exp(v2 - v1)
    w1 = 1.0 / (1.0 + e)
    w2 = e / (1.0 + e)
    out_lane = lax.broadcasted_iota(jnp.int32, (TM, 8), 1)
    r = jnp.where(out_lane == 0, i1.astype(F32), 0.0)
    r = jnp.where(out_lane == 1, i2.astype(F32), r)
    r = jnp.where(out_lane == 2, w1, r)
    r = jnp.where(out_lane == 3, w2, r)
    r_out[...] = r


RUN_ALIGN = 8
RUN_BITS = tuple(1 << b for b in range(TM.bit_length() - 1, RUN_ALIGN.bit_length() - 2, -1))
LOCAL_ROWS = 2 * TM + 64


def _run_copies(run_ref, local, remote, sem, *, to_remote, wait):
    for e in range(N_EXPERTS):
        l0, n, g0 = run_ref[0, 0, e], run_ref[0, 1, e], run_ref[0, 2, e]
        done = jnp.int32(0)
        for bit in RUN_BITS:
            take = (n & bit) != 0

            @pl.when(take)
            def _(done=done, bit=bit):
                loc = local.at[pl.ds(pl.multiple_of(l0 + done, RUN_ALIGN), bit)]
                rem = remote.at[pl.ds(pl.multiple_of(g0 + done, RUN_ALIGN), bit)]
                cp = pltpu.make_async_copy(loc, rem, sem) if to_remote else pltpu.make_async_copy(rem, loc, sem)
                if wait:
                    cp.wait()
                else:
                    cp.start()

            done = done + (n & bit)


def _block_onehot(lp, shape, axis):
    return jnp.where(lax.broadcasted_iota(jnp.int32, shape, axis) == lp, 1.0, 0.0).astype(BF16)


def _dispatch_kernel(run_ref, x_ref, mod_ref, g_ref, lp_ref, init_hbm, xs_hbm, sbuf, sem):
    del init_hbm
    h = _modulate(x_ref[...], g_ref[...], mod_ref[3:4, :], mod_ref[4:5, :]).astype(BF16)
    lp = lp_ref[...]
    place = _block_onehot(lp[0:1, :], (LOCAL_ROWS, TM), 0) + _block_onehot(lp[1:2, :], (LOCAL_ROWS, TM), 0)
    sbuf[...] = jnp.dot(place, h, preferred_element_type=F32)
    _run_copies(run_ref, sbuf, xs_hbm, sem, to_remote=True, wait=False)
    _run_copies(run_ref, sbuf, xs_hbm, sem, to_remote=True, wait=True)


def _expert_kernel(meta_ref, xs_ref, wgu_ref, wd_ref, y_ref):
    t = pl.program_id(0)
    n_used = meta_ref[pl.num_programs(0)]

    @pl.when(t < n_used)
    def _():
        y_ref[...] = _swiglu_rows(xs_ref[...].astype(BF16), wgu_ref, wd_ref)

    @pl.when(t >= n_used)
    def _():
        y_ref[...] = jnp.zeros_like(y_ref)


def _combine_kernel(run_ref, y_hbm, x_ref, mod_ref, r_ref, lp_ref, fin_ref, o_ref, ybuf, sem, *, final):
    ybuf[2 * TM:LOCAL_ROWS, :] = jnp.zeros((LOCAL_ROWS - 2 * TM, D), F32)
    _run_copies(run_ref, ybuf, y_hbm, sem, to_remote=False, wait=False)
    _run_copies(run_ref, ybuf, y_hbm, sem, to_remote=False, wait=True)
    ys = ybuf[...].astype(BF16)
    lp = lp_ref[...]
    y0 = jnp.dot(_block_onehot(lp[:, 0:1], (TM, LOCAL_ROWS), 1), ys, preferred_element_type=F32)
    y1 = jnp.dot(_block_onehot(lp[:, 1:2], (TM, LOCAL_ROWS), 1), ys, preferred_element_type=F32)
    r = r_ref[...]
    x = x_ref[...] + mod_ref[5:6, :] * (r[:, 2:3] * y0 + r[:, 3:4] * y1)
    if final:
        x = (x * lax.rsqrt(jnp.mean(x * x, axis=-1, keepdims=True) + EPS)) * fin_ref[...]
    o_ref[...] = x


def _moe_ffn(x, mods, norm_g, w_router, w_gu, w_down, n_ctx, final_g=None):
    B, T, _ = x.shape
    nt = T // TM
    nct = n_ctx // TM
    n_tok = B * T
    nblk = B * nt
    wr = jnp.zeros((D, 128), F32).at[:, :N_EXPERTS].set(w_router)
    route = pl.pallas_call(
        _router_kernel,
        out_shape=_sds((B, T, 8), F32),
        grid=(B, nt),
        in_specs=[_x_spec(), _mod_spec(nct), _const_spec((1, D)), _const_spec((D, 128))],
        out_specs=pl.BlockSpec((None, TM, 8), lambda b, i: (b, i, 0)),
        compiler_params=_params(("parallel", "parallel"), 32),
        name="router",
    )(x, mods, norm_g.reshape(1, D), wr)

    ids = jnp.arange(N_EXPERTS, dtype=jnp.int32)
    eid = route.reshape(nblk, TM, 8)[:, :, :2].astype(jnp.int32).reshape(nblk, 2 * TM)
    onehot = (eid[:, :, None] == ids).astype(jnp.int32)
    cnt = jnp.sum(onehot, axis=1)
    cnt = ((cnt + RUN_ALIGN - 1) // RUN_ALIGN) * RUN_ALIGN
    l0 = jnp.cumsum(cnt, axis=1) - cnt
    rank = jnp.sum((jnp.cumsum(onehot, axis=1) - onehot) * onehot, axis=2)
    lp = jnp.sum(onehot * l0[:, None, :], axis=2) + rank
    counts = jnp.sum(cnt, axis=0)
    padded = ((counts + TME - 1) // TME) * TME
    ends = jnp.cumsum(padded)
    g0 = (ends - padded)[None, :] + jnp.cumsum(cnt, axis=0) - cnt
    runs = jnp.stack([l0, cnt, g0], axis=1)
    lp_cols = lp.reshape(nblk, TM, 2)
    lp_rows = lp_cols.transpose(0, 2, 1)
    n_tiles = -(-(2 * n_tok + nblk * N_EXPERTS * (RUN_ALIGN - 1) + N_EXPERTS * (TME - 1)) // TME)
    n_rows = n_tiles * TME
    tile_row = jnp.arange(n_tiles, dtype=jnp.int32)[:, None] * TME
    tile_exp = jnp.minimum(jnp.sum((ends[None, :] <= tile_row).astype(jnp.int32), axis=1), N_EXPERTS - 1)
    meta = jnp.concatenate([tile_exp, (ends[-1:] // TME).astype(jnp.int32)])

    run_spec = lambda off: pl.BlockSpec((1, 3, N_EXPERTS), lambda b, i: (b * nt + i + off, 0, 0),
                                        memory_space=pltpu.SMEM)
    xs = pl.pallas_call(
        _dispatch_kernel,
        out_shape=_sds((n_rows, D), F32),
        grid=(B, nt),
        in_specs=[run_spec(0), _x_spec(), _mod_spec(nct), _const_spec((1, D)),
                  pl.BlockSpec((None, 2, TM), lambda b, i: (b * nt + i, 0, 0)),
                  pl.BlockSpec(memory_space=pl.ANY)],
        out_specs=pl.BlockSpec(memory_space=pl.ANY),
        scratch_shapes=[pltpu.VMEM((LOCAL_ROWS, D), F32), pltpu.SemaphoreType.DMA(())],
        input_output_aliases={5: 0},
        compiler_params=_params(("arbitrary", "arbitrary"), 40),
        name="moe_dispatch",
    )(runs, x, mods, norm_g.reshape(1, D), lp_rows, jnp.zeros((n_rows, D), F32))

    y = pl.pallas_call(
        _expert_kernel,
        out_shape=_sds((n_rows, D), F32),
        grid_spec=pltpu.PrefetchScalarGridSpec(
            num_scalar_prefetch=1,
            grid=(n_tiles,),
            in_specs=[pl.BlockSpec((TME, D), lambda t, meta: (t, 0)),
                      pl.BlockSpec((None, D, 2 * D_FF), lambda t, meta: (meta[t], 0, 0)),
                      pl.BlockSpec((None, D_FF, D), lambda t, meta: (meta[t], 0, 0))],
            out_specs=pl.BlockSpec((TME, D), lambda t, meta: (t, 0))),
        compiler_params=_params(("arbitrary",), 56),
        name="experts",
    )(meta, xs, w_gu.astype(BF16), w_down.astype(BF16))

    final = final_g is not None
    off = nct if final else 0
    tok_spec = lambda last: pl.BlockSpec((None, TM, last), lambda b, i: (b, i + off, 0))
    return pl.pallas_call(
        functools.partial(_combine_kernel, final=final),
        out_shape=_sds((B, T - off * TM, D), F32),
        grid=(B, nt - off),
        in_specs=[run_spec(off), pl.BlockSpec(memory_space=pl.ANY), tok_spec(D),
                  pl.BlockSpec((None, None, 6, D), lambda b, i: (b, jnp.where(i + off >= nct, 1, 0), 0, 0)),
                  tok_spec(8),
                  pl.BlockSpec((None, TM, 2), lambda b, i: (b * nt + i + off, 0, 0)),
                  _const_spec((1, D))],
        out_specs=_x_spec(),
        scratch_shapes=[pltpu.VMEM((LOCAL_ROWS, D), F32), pltpu.SemaphoreType.DMA(())],
        compiler_params=_params(("arbitrary", "arbitrary"), 40),
        name="moe_combine",
    )(runs, y, x, mods, route, lp_cols, (final_g if final else norm_g).reshape(1, D))


def kernel(x, c, ctx, c_ctx, l0_ada_w, l0_ada_b, l0_norm_mix, l0_norm_ffn, l0_conv_w_in, l0_conv_w, l0_conv_w_out, l0_ffn_w_gu, l0_ffn_w_down, l1_ada_w, l1_ada_b, l1_norm_mix, l1_norm_ffn, l1_attn_w_qkv, l1_attn_lam, l1_attn_subln, l1_attn_w_o, l1_moe_router, l1_moe_w_gu, l1_moe_w_down, l2_ada_w, l2_ada_b, l2_norm_mix, l2_norm_ffn, l2_ssm_a_re, l2_ssm_a_im, l2_ssm_log_dt, l2_ssm_b_re, l2_ssm_b_im, l2_ssm_c_re, l2_ssm_c_im, l2_ssm_d, l2_ssm_w_glu, l2_ffn_w_gu, l2_ffn_w_down, l3_ada_w, l3_ada_b, l3_norm_mix, l3_norm_ffn, l3_conv_w_in, l3_conv_w, l3_conv_w_out, l3_moe_router, l3_moe_w_gu, l3_moe_w_down, final_norm):
    B, n_lat, _ = x.shape
    n_ctx = ctx.shape[1]
    T = n_ctx + n_lat
    assert n_ctx % TM == 0 and n_lat % TM == 0 and n_lat % GRID_W == 0 and B < 8
    xs = jnp.concatenate([ctx, x], axis=1)
    cond8 = jnp.concatenate([c, c_ctx[None], jnp.zeros((8 - B - 1, D), F32)], axis=0)

    def mods_of(w, b):
        m = _ada(cond8, w, b).reshape(8, 6, D)
        return jnp.stack([jnp.broadcast_to(m[B], (B, 6, D)), m[:B]], axis=1)

    mods = mods_of(l0_ada_w, l0_ada_b)
    xs = _conv_mixer(xs, mods, l0_norm_mix, l0_conv_w_in, l0_conv_w, l0_conv_w_out, n_ctx)
    xs = _dense_ffn(xs, mods, l0_norm_ffn, l0_ffn_w_gu, l0_ffn_w_down, n_ctx)

    mods = mods_of(l1_ada_w, l1_ada_b)
    xs = _attn_mixer(xs, mods, l1_norm_mix, l1_attn_w_qkv, l1_attn_lam, l1_attn_subln, l1_attn_w_o, n_ctx,
                     0.8 - 0.6 * math.exp(-0.3 * 1))
    xs = _moe_ffn(xs, mods, l1_norm_ffn, l1_moe_router, l1_moe_w_gu, l1_moe_w_down, n_ctx)

    mods = mods_of(l2_ada_w, l2_ada_b)
    xs = _s5_mixer(xs, mods, l2_norm_mix,
                   (l2_ssm_a_re, l2_ssm_a_im, l2_ssm_log_dt, l2_ssm_b_re, l2_ssm_b_im, l2_ssm_c_re, l2_ssm_c_im,
                    l2_ssm_d, l2_ssm_w_glu), n_ctx)
    xs = _dense_ffn(xs, mods, l2_norm_ffn, l2_ffn_w_gu, l2_ffn_w_down, n_ctx)

    mods = mods_of(l3_ada_w, l3_ada_b)
    xs = _conv_mixer(xs, mods, l3_norm_mix, l3_conv_w_in, l3_conv_w, l3_conv_w_out, n_ctx)
    return _moe_ffn(xs, mods, l3_norm_ffn, l3_moe_router, l3_moe_w_gu, l3_moe_w_down, n_ctx, final_g=final_norm)
```

```python
import functools
import math

import jax
import jax.numpy as jnp
from jax import lax
from jax.experimental import pallas as pl
from jax.experimental.pallas import tpu as pltpu

F32 = jnp.float32
BF16 = jnp.bfloat16
HIGHEST = lax.Precision.HIGHEST

D = 1024
GRID_W = 64
N_HEADS = 8
HEAD_DIM = 64
V_DIM = 128
ROPE_BASE = 10000.0
GROUP_SIZE = 16
N_GROUPS = 64
STATE_DIM = 64
D_FF = 2816
N_EXPERTS = 8
EPS = 1e-6

TM = 256
FF_CHUNK = 256
TK = 256
S5_T = 16
TME = 512
MIB = 1 << 20


def _sds(shape, dtype):
    return jax.ShapeDtypeStruct(shape, dtype)


def _params(sem, vmem_mib):
    return pltpu.CompilerParams(dimension_semantics=sem, vmem_limit_bytes=vmem_mib * MIB)


def _modulate(x, g, shift, scale):
    y = x * lax.rsqrt(jnp.mean(x * x, axis=-1, keepdims=True) + EPS)
    return (y * g) * (1.0 + scale) + shift


def _x_spec():
    return pl.BlockSpec((None, TM, D), lambda b, i: (b, i, 0))


def _mod_spec(n_ctx_tiles):
    return pl.BlockSpec((None, None, 6, D), lambda b, i: (b, jnp.where(i >= n_ctx_tiles, 1, 0), 0, 0))


def _const_spec(shape):
    nd = len(shape)
    return pl.BlockSpec(shape, lambda b, i: (0,) * nd)


def _ada_kernel(c_ref, w_ref, b_ref, o_ref):
    c = c_ref[...]
    s = c * jax.nn.sigmoid(c)
    o_ref[...] = jnp.dot(s, w_ref[...], preferred_element_type=F32, precision=HIGHEST) + b_ref[...]


def _ada(cond8, w, b):
    nb = 4
    bn = w.shape[1] // nb
    return pl.pallas_call(
        _ada_kernel,
        out_shape=_sds((8, w.shape[1]), F32),
        grid=(nb,),
        in_specs=[pl.BlockSpec((8, D), lambda j: (0, 0)),
                  pl.BlockSpec((D, bn), lambda j: (0, j)),
                  pl.BlockSpec((1, bn), lambda j: (0, j))],
        out_specs=pl.BlockSpec((8, bn), lambda j: (0, j)),
        compiler_params=_params(("parallel",), 40),
        name="ada",
    )(cond8, w, b.reshape(1, -1))


def _conv_in_kernel(x_ref, mod_ref, g_ref, w_ref, b_out, z_out):
    h = _modulate(x_ref[...], g_ref[...], mod_ref[0:1, :], mod_ref[1:2, :]).astype(BF16)
    b_out[...] = jnp.dot(h, w_ref[:, 0:D], preferred_element_type=F32).astype(BF16)
    c_gate = jnp.dot(h, w_ref[:, D:2 * D], preferred_element_type=F32)
    v = jnp.dot(h, w_ref[:, 2 * D:3 * D], preferred_element_type=F32)
    z_out[...] = c_gate * v


def _conv_out_kernel(z_ref, zp_ref, zn_ref, b_ref, cw_ref, w_ref, x_ref, mod_ref, o_ref, *, n_ctx, n_tot):
    i = pl.program_id(1)
    z = z_ref[...]
    row = lax.broadcasted_iota(jnp.int32, (TM, 1), 0)
    grow = row + i * TM
    prev_row = zp_ref[7:8, :]
    next_row = zn_ref[0:1, :]
    zm1 = jnp.where(row == 0, prev_row, pltpu.roll(z, 1, axis=0))
    zp1 = jnp.where(row == TM - 1, next_row, pltpu.roll(z, TM - 1, axis=0))
    zm1 = jnp.where((grow == 0) | (grow == n_ctx), 0.0, zm1)
    zp1 = jnp.where((grow == n_ctx - 1) | (grow == n_tot - 1), 0.0, zp1)
    y = cw_ref[0:1, :] * zm1 + cw_ref[1:2, :] * z + cw_ref[2:3, :] * zp1
    a = (b_ref[...].astype(F32) * y).astype(BF16)
    o_ref[...] = x_ref[...] + mod_ref[2:3, :] * jnp.dot(a, w_ref[...], preferred_element_type=F32)


def _conv_mixer(x, mods, norm_g, w_in, conv_w, w_out, n_ctx):
    B, T, _ = x.shape
    nt = T // TM
    nct = n_ctx // TM
    b_gate, z = pl.pallas_call(
        _conv_in_kernel,
        out_shape=(_sds((B, T, D), BF16), _sds((B, T, D), F32)),
        grid=(B, nt),
        in_specs=[_x_spec(), _mod_spec(nct), _const_spec((1, D)), _const_spec((D, 3 * D))],
        out_specs=(_x_spec(), _x_spec()),
        compiler_params=_params(("parallel", "parallel"), 48),
        name="conv_in",
    )(x, mods, norm_g.reshape(1, D), w_in.astype(BF16))
    r8 = TM // 8
    return pl.pallas_call(
        functools.partial(_conv_out_kernel, n_ctx=n_ctx, n_tot=T),
        out_shape=_sds((B, T, D), F32),
        grid=(B, nt),
        in_specs=[_x_spec(),
                  pl.BlockSpec((None, 8, D), lambda b, i: (b, jnp.maximum(i * r8 - 1, 0), 0)),
                  pl.BlockSpec((None, 8, D), lambda b, i: (b, jnp.minimum((i + 1) * r8, T // 8 - 1), 0)),
                  _x_spec(), _const_spec((3, D)), _const_spec((D, D)), _x_spec(), _mod_spec(nct)],
        out_specs=_x_spec(),
        compiler_params=_params(("parallel", "parallel"), 48),
        name="conv_out",
    )(z, z, z, b_gate, conv_w, w_out.astype(BF16), x, mods)


def _qkv_kernel(x_ref, mod_ref, g_ref, w_ref, cos_ref, slo_ref, shi_ref, qt_out, k_out, vt_out):
    h = _modulate(x_ref[...], g_ref[...], mod_ref[0:1, :], mod_ref[1:2, :]).astype(BF16)
    cos = cos_ref[...]
    s_lo = slo_ref[...]
    s_hi = shi_ref[...]

    def rope(t):
        return t * cos + pltpu.roll(t, 128 - 16, axis=1) * s_lo + pltpu.roll(t, 16, axis=1) * s_hi

    q_scale = (HEAD_DIM ** -0.5) * math.log2(math.e)
    for hd in range(N_HEADS):
        lo, hi = hd * 128, (hd + 1) * 128
        q = jnp.dot(h, w_ref[:, lo:hi], preferred_element_type=F32) * q_scale
        qt_out[hd] = rope(q).T.astype(BF16)
        k = jnp.dot(h, w_ref[:, D + lo:D + hi], preferred_element_type=F32)
        k_out[:, lo:hi] = rope(k).astype(BF16)
        v = jnp.dot(h, w_ref[:, 2 * D + lo:2 * D + hi], preferred_element_type=F32)
        vt_out[hd] = v.astype(BF16).astype(F32).T.astype(BF16)


def _attn_kernel(qt_ref, k_ref, vt_ref, lam_ref, sub_ref, o_ref, s_buf, acc_ref, *,
                 n_ctx_tiles, n_lat_chunks, lam_init):
    i = pl.program_id(2)
    qt = qt_ref[...]
    row = lax.broadcasted_iota(jnp.int32, qt.shape, 0)
    zero = jnp.zeros_like(qt)
    q_cat = jnp.concatenate([jnp.where(row < HEAD_DIM, qt, zero), jnp.where(row >= HEAD_DIM, qt, zero)], axis=1)

    def scores(first_tile, n_tiles):
        start = first_tile * TM
        if not isinstance(first_tile, int):
            start = pl.multiple_of(start, TM)
        s = jnp.dot(k_ref[pl.ds(start, n_tiles * TM), :], q_cat, preferred_element_type=F32)
        return s, jnp.max(s, axis=0, keepdims=True)

    def absorb(s, s_max, first_tile, n_tiles, m, l):
        m_new = jnp.maximum(m, s_max)
        p = jnp.exp2(s - m_new)
        alpha = jnp.exp2(m - m_new)
        l_new = alpha * l + jnp.sum(p, axis=0, keepdims=True)
        pb = p.astype(BF16)
        pv = jnp.dot(vt_ref[first_tile], pb[0:TM, :], preferred_element_type=F32)
        for c in range(1, n_tiles):
            pv = pv + jnp.dot(vt_ref[first_tile + c], pb[c * TM:(c + 1) * TM, :], preferred_element_type=F32)
        acc_ref[...] = alpha * acc_ref[...] + pv
        return m_new, l_new

    def finish(l):
        acc = acc_ref[...]
        lv = lam_ref[...]
        lam = (jnp.exp(jnp.sum(lv[0:1, :] * lv[1:2, :], axis=-1, keepdims=True))
               - jnp.exp(jnp.sum(lv[2:3, :] * lv[3:4, :], axis=-1, keepdims=True)) + lam_init)
        o = (acc[:, 0:TM] / l[:, 0:TM] - lam * (acc[:, TM:2 * TM] / l[:, TM:2 * TM])).T
        o = o * lax.rsqrt(jnp.mean(o * o, axis=-1, keepdims=True) + EPS)
        o_ref[...] = ((o * sub_ref[...]) * (1.0 - lam_init)).astype(BF16)

    acc_ref[...] = jnp.zeros_like(acc_ref)
    s_ctx, max_ctx = scores(0, n_ctx_tiles)
    s_buf[0], max_first = scores(n_ctx_tiles, 2)
    m_ctx, l_ctx = absorb(s_ctx, max_ctx, 0, n_ctx_tiles,
                          jnp.full((1, 2 * TM), -1e30, F32), jnp.zeros((1, 2 * TM), F32))

    @pl.when(i < n_ctx_tiles)
    def _():
        finish(l_ctx)

    @pl.when(i >= n_ctx_tiles)
    def _():
        m, l, maxes = m_ctx, l_ctx, [max_first, None]
        for c in range(n_lat_chunks):
            if c + 1 < n_lat_chunks:
                s_buf[(c + 1) % 2], maxes[(c + 1) % 2] = scores(n_ctx_tiles + 2 * (c + 1), 2)
            m, l = absorb(s_buf[c % 2], maxes[c % 2], n_ctx_tiles + 2 * c, 2, m, l)
        finish(l)


def _out_proj_kernel(a_ref, w_ref, x_ref, mod_ref, o_ref):
    o_ref[...] = x_ref[...] + mod_ref[2:3, :] * jnp.dot(a_ref[...], w_ref[...], preferred_element_type=F32)


def _rope_tables(n_ctx, n_lat):
    t = jnp.arange(n_lat, dtype=jnp.int32)
    row = (t // GRID_W).astype(F32)
    col = (t % GRID_W).astype(F32)
    n_freq = HEAD_DIM // 4
    inv = ROPE_BASE ** (-jnp.arange(n_freq, dtype=F32) / n_freq)
    ang_r = row[:, None] * inv
    ang_c = col[:, None] * inv
    zero = jnp.zeros_like(ang_r)
    cos64 = jnp.concatenate([jnp.cos(ang_r), jnp.cos(ang_r), jnp.cos(ang_c), jnp.cos(ang_c)], axis=-1)
    slo64 = jnp.concatenate([-jnp.sin(ang_r), zero, -jnp.sin(ang_c), zero], axis=-1)
    shi64 = jnp.concatenate([zero, jnp.sin(ang_r), zero, jnp.sin(ang_c)], axis=-1)

    def full(tab, ctx_val):
        tab = jnp.concatenate([tab, tab], axis=-1)
        return jnp.concatenate([jnp.full((n_ctx, 128), ctx_val, F32), tab], axis=0)

    return full(cos64, 1.0), full(slo64, 0.0), full(shi64, 0.0)


def _attn_mixer(x, mods, norm_g, w_qkv, lam_vecs, subln, w_o, n_ctx, lam_init):
    B, T, _ = x.shape
    nt = T // TM
    nct = n_ctx // TM
    cos, s_lo, s_hi = _rope_tables(n_ctx, T - n_ctx)
    tab_spec = pl.BlockSpec((TM, 128), lambda b, i: (i, 0))
    assert (T - n_ctx) % (4 * TM) == 0
    t_spec = pl.BlockSpec((None, N_HEADS, None, 128, TM), lambda b, i: (b, 0, i, 0, 0))
    t_shape = _sds((B, N_HEADS, nt, 128, TM), BF16)
    qt, k, vt = pl.pallas_call(
        _qkv_kernel,
        out_shape=(t_shape, _sds((B, T, D), BF16), t_shape),
        grid=(B, nt),
        in_specs=[_x_spec(), _mod_spec(nct), _const_spec((1, D)), _const_spec((D, 3 * D)),
                  tab_spec, tab_spec, tab_spec],
        out_specs=(t_spec, _x_spec(), t_spec),
        compiler_params=_params(("parallel", "parallel"), 48),
        name="qkv",
    )(x, mods, norm_g.reshape(1, D), w_qkv.astype(BF16), cos, s_lo, s_hi)

    attn = pl.pallas_call(
        functools.partial(_attn_kernel, n_ctx_tiles=nct, n_lat_chunks=(T - n_ctx) // (2 * TM), lam_init=lam_init),
        out_shape=_sds((B, T, D), BF16),
        grid=(B, N_HEADS, nt),
        in_specs=[pl.BlockSpec((None, None, None, 128, TM), lambda b, h, i: (b, h, i, 0, 0)),
                  pl.BlockSpec((None, T, 128), lambda b, h, i: (b, 0, h)),
                  pl.BlockSpec((None, None, nt, 128, TM), lambda b, h, i: (b, h, 0, 0, 0)),
                  pl.BlockSpec((4, HEAD_DIM), lambda b, h, i: (0, 0)),
                  pl.BlockSpec((1, V_DIM), lambda b, h, i: (0, 0))],
        out_specs=pl.BlockSpec((None, TM, 128), lambda b, h, i: (b, i, h)),
        scratch_shapes=[pltpu.VMEM((2, 2 * TM, 2 * TM), F32), pltpu.VMEM((V_DIM, 2 * TM), F32)],
        compiler_params=_params(("parallel", "parallel", "parallel"), 48),
        name="diff_attn",
    )(qt, k, vt, lam_vecs, subln.reshape(1, V_DIM))

    return pl.pallas_call(
        _out_proj_kernel,
        out_shape=_sds((B, T, D), F32),
        grid=(B, nt),
        in_specs=[_x_spec(), _const_spec((D, D)), _x_spec(), _mod_spec(nct)],
        out_specs=_x_spec(),
        compiler_params=_params(("parallel", "parallel"), 48),
        name="attn_out",
    )(attn, w_o.astype(BF16), x, mods)


def _mod_only_kernel(x_ref, mod_ref, g_ref, h_out):
    h_out[...] = _modulate(x_ref[...], g_ref[...], mod_ref[0:1, :], mod_ref[1:2, :])


def _s5_chunk_rows(h_ref):
    return jnp.concatenate([h_ref[:, s, :] for s in range(S5_T)], axis=-1).astype(BF16)


def _s5_state_kernel(h_ref, w_ref, sfr, sfi, sbr, sbi):
    s = jnp.dot(_s5_chunk_rows(h_ref), w_ref[...], preferred_element_type=F32)
    sfr[...] = s[:, 0:512]
    sfi[...] = s[:, 512:1024]
    sbr[...] = s[:, 1024:1536]
    sbi[...] = s[:, 1536:2048]


def _s5_scan_kernel(sfr, sfi, sbr, sbi, afr, afi, abr, abi, hfr, hfi, hbr, hbi, *, n_ctx_chunks, n_chunks):
    ar, ai = afr[...], afi[...]

    def fwd(n, carry):
        hr, hi = carry
        hfr[n] = hr
        hfi[n] = hi
        sr, si = sfr[n], sfi[n]
        return ar * hr - ai * hi + sr, ar * hi + ai * hr + si

    zero = jnp.zeros(ar.shape, F32)
    lax.fori_loop(0, n_chunks, fwd, (zero, zero))

    br, bi = abr[...], abi[...]

    def bwd(t, carry, top):
        n = top - t
        hr, hi = carry
        hbr[n] = hr
        hbi[n] = hi
        sr, si = sbr[n], sbi[n]
        return br * hr - bi * hi + sr, br * hi + bi * hr + si

    carry = lax.fori_loop(0, n_ctx_chunks, functools.partial(bwd, top=n_ctx_chunks - 1), (zero, zero))
    lax.fori_loop(0, n_chunks - n_ctx_chunks, functools.partial(bwd, top=n_chunks - 1), carry)


def _s5_out_kernel(h_ref, m_ref, hfr, hfi, hbr, hbi, v_ref, y_ref):
    state = jnp.concatenate([hfr[...], hfi[...], hbr[...], hbi[...]], axis=-1).astype(BF16)
    y = (jnp.dot(_s5_chunk_rows(h_ref), m_ref[...], preferred_element_type=F32)
         + jnp.dot(state, v_ref[...], preferred_element_type=F32))
    for t in range(S5_T // 2):
        y_ref[:, t, :] = y[:, t * 128:(t + 1) * 128]


def _glu_kernel(y_ref, x_ref, mod_ref, g_ref, d_ref, w_ref, o_ref):
    x = x_ref[...]
    hl = _modulate(x, g_ref[...], mod_ref[0:1, :], mod_ref[1:2, :])
    y = y_ref[...] + d_ref[...] * hl
    a = jax.nn.gelu(y).astype(BF16)
    val = jnp.dot(a, w_ref[:, 0:D], preferred_element_type=F32)
    gate = jnp.dot(a, w_ref[:, D:2 * D], preferred_element_type=F32)
    o_ref[...] = x + mod_ref[2:3, :] * (val * jax.nn.sigmoid(gate))


def _s5_matrices(a_re, a_im, log_dt, b_re, b_im, c_re, c_im):
    T = S5_T
    hp = functools.partial(jnp.einsum, precision=HIGHEST)
    pw_re, pw_im, bb_re, bb_im = [], [], [], []
    for d in range(2):
        ar, ai = a_re[d].astype(F32), a_im[d].astype(F32)
        dt = jnp.exp(log_dt[d].astype(F32))[:, None]
        mag = jnp.exp(dt * ar)
        ab_re = mag * jnp.cos(dt * ai)
        ab_im = mag * jnp.sin(dt * ai)
        den = ar * ar + ai * ai
        nr = ab_re - 1.0
        co_re = (nr * ar + ab_im * ai) / den
        co_im = (ab_im * ar - nr * ai) / den
        br, bi = b_re[d].astype(F32), b_im[d].astype(F32)
        bb_re.append(co_re[..., None] * br - co_im[..., None] * bi)
        bb_im.append(co_re[..., None] * bi + co_im[..., None] * br)
        pr = [jnp.ones_like(ab_re)]
        pi = [jnp.zeros_like(ab_re)]
        for _ in range(T):
            pr.append(pr[-1] * ab_re - pi[-1] * ab_im)
            pi.append(pr[-2] * ab_im + pi[-1] * ab_re)
        pw_re.append(jnp.stack(pr))
        pw_im.append(jnp.stack(pi))

    def lag_kernels(d):
        mr = pw_re[d][:T, :, :, None] * bb_re[d][None] - pw_im[d][:T, :, :, None] * bb_im[d][None]
        mi = pw_re[d][:T, :, :, None] * bb_im[d][None] + pw_im[d][:T, :, :, None] * bb_re[d][None]
        return hp('gop,kgpi->kgoi', c_re[d].astype(F32), mr) - hp('gop,kgpi->kgoi', c_im[d].astype(F32), mi)

    kf, kb = lag_kernels(0), lag_kernels(1)
    lag0 = (kf[0] + kb[0])[None]
    kfull = jnp.concatenate([kb[:0:-1], lag0, kf[1:]], axis=0)
    tt = jnp.arange(T)
    idx = tt[None, :] - tt[:, None] + (T - 1)
    mt = kfull[idx]
    mt = mt.transpose(2, 0, 4, 1, 3).reshape(N_GROUPS, T * GROUP_SIZE, T * GROUP_SIZE)

    def end_state(d, powers):
        pr, pi = pw_re[d][powers], pw_im[d][powers]
        wr = pr[..., None] * bb_re[d][None] - pi[..., None] * bb_im[d][None]
        wi = pr[..., None] * bb_im[d][None] + pi[..., None] * bb_re[d][None]
        f = lambda w: w.transpose(1, 0, 3, 2).reshape(N_GROUPS, T * GROUP_SIZE, STATE_DIM)
        return f(wr), f(wi)

    def read_out(d, powers):
        pr, pi = pw_re[d][powers], pw_im[d][powers]
        cr, ci = c_re[d].astype(F32), c_im[d].astype(F32)
        qr = cr[None] * pr[:, :, None, :] - ci[None] * pi[:, :, None, :]
        qi = cr[None] * pi[:, :, None, :] + ci[None] * pr[:, :, None, :]
        f = lambda q: q.transpose(1, 3, 0, 2).reshape(N_GROUPS, STATE_DIM, T * GROUP_SIZE)
        return f(qr), f(-qi)

    wfr, wfi = end_state(0, T - 1 - tt)
    wbr, wbi = end_state(1, tt)
    vfr, vfi = read_out(0, tt + 1)
    vbr, vbi = read_out(1, T - tt)

    gsz = T * GROUP_SIZE
    g8 = jnp.arange(8, dtype=jnp.int32)[:, None]
    k = jnp.arange(gsz, dtype=jnp.int32)[None, :]
    chunk_col = (k // GROUP_SIZE) * 128 + g8 * GROUP_SIZE + k % GROUP_SIZE
    state_col = (k // STATE_DIM) * (8 * STATE_DIM) + g8 * STATE_DIM + k % STATE_DIM
    lanes = jnp.arange(8 * gsz, dtype=jnp.int32)[None, None, :]
    to_chunk = (chunk_col[:, :, None] == lanes).astype(BF16)
    to_state = (state_col[:, :, None] == lanes).astype(BF16)
    w_cat = jnp.stack([wfr, wfi, wbr, wbi], axis=2)
    v_cat = jnp.stack([vfr, vfi, vbr, vbi], axis=1)
    m8 = _spread(mt, to_chunk, T, GROUP_SIZE)
    w8 = _spread(w_cat.reshape(N_GROUPS, gsz, gsz), to_state, T, GROUP_SIZE)
    v8 = _spread(v_cat.reshape(N_GROUPS, gsz, gsz), to_chunk, 4, STATE_DIM)
    a_pow = [t[T].reshape(N_GROUPS // 2, 2 * STATE_DIM) for t in (pw_re[0], pw_im[0], pw_re[1], pw_im[1])]
    return m8, w8, v8, a_pow


def _spread_kernel(a_ref, c_ref, o_ref):
    e = jnp.dot(a_ref[...], c_ref[...], preferred_element_type=F32)
    o_ref[...] = e.reshape(o_ref.shape).astype(BF16)


def _spread(a, col_of, r1, r2):
    nb = N_GROUPS // 8
    kdim, n = col_of.shape[1], col_of.shape[2]
    out = pl.pallas_call(
        _spread_kernel,
        out_shape=_sds((nb, r1, 8, r2, n), BF16),
        grid=(nb, 8),
        in_specs=[pl.BlockSpec((None, None, r1 * r2, kdim), lambda g, j: (g, j, 0, 0)),
                  pl.BlockSpec((None, kdim, n), lambda g, j: (j, 0, 0))],
        out_specs=pl.BlockSpec((None, r1, None, r2, n), lambda g, j: (g, 0, j, 0, 0)),
        compiler_params=_params(("parallel", "parallel"), 32),
        name="s5_spread",
    )(a.astype(BF16).reshape(nb, 8, r1 * r2, kdim), col_of)
    return out.reshape(nb, r1 * 8 * r2, n)


def _s5_mixer(x, mods, norm_g, ssm, n_ctx):
    a_re, a_im, log_dt, b_re, b_im, c_re, c_im, d_skip, w_glu = ssm
    B, T, _ = x.shape
    nt = T // TM
    nct = n_ctx // TM
    nch = T // S5_T
    nb = N_GROUPS // 8
    sw = 8 * STATE_DIM
    cw = S5_T * 128
    m8, w8, v8, a_pow = _s5_matrices(a_re, a_im, log_dt, b_re, b_im, c_re, c_im)

    h = pl.pallas_call(
        _mod_only_kernel,
        out_shape=_sds((B, T, D), F32),
        grid=(B, nt),
        in_specs=[_x_spec(), _mod_spec(nct), _const_spec((1, D))],
        out_specs=_x_spec(),
        compiler_params=_params(("parallel", "parallel"), 32),
        name="s5_mod",
    )(x, mods, norm_g.reshape(1, D))
    h4 = h.reshape(B * nch, S5_T, D)

    rows = B * nch
    state_shape = _sds((rows, nb * sw), F32)
    s_parts = pl.pallas_call(
        _s5_state_kernel,
        out_shape=(state_shape,) * 4,
        grid=(nb, B),
        in_specs=[pl.BlockSpec((nch, S5_T, 128), lambda g, b: (b, 0, g)),
                  pl.BlockSpec((None, cw, 4 * sw), lambda g, b: (g, 0, 0))],
        out_specs=(pl.BlockSpec((nch, sw), lambda g, b: (b, g)),) * 4,
        compiler_params=_params(("parallel", "parallel"), 56),
        name="s5_state",
    )(h4, w8)

    n_pairs = N_GROUPS // 2
    npb = 8
    view = lambda t: t.reshape(B, nch, n_pairs, 128)
    scan_blk = pl.BlockSpec((None, nch, npb, 128), lambda b, j: (b, 0, j, 0))
    coef_blk = pl.BlockSpec((npb, 128), lambda b, j: (j, 0))
    h_parts = pl.pallas_call(
        functools.partial(_s5_scan_kernel, n_ctx_chunks=n_ctx // S5_T, n_chunks=nch),
        out_shape=(_sds((B, nch, n_pairs, 128), F32),) * 4,
        grid=(B, n_pairs // npb),
        in_specs=[scan_blk] * 4 + [coef_blk] * 4,
        out_specs=(scan_blk,) * 4,
        compiler_params=_params(("parallel", "parallel"), 56),
        name="s5_scan",
    )(*[view(t) for t in s_parts], *a_pow)

    half = S5_T // 2
    y4 = pl.pallas_call(
        _s5_out_kernel,
        out_shape=_sds((rows, S5_T, D), F32),
        grid=(nb, 2, B),
        in_specs=[pl.BlockSpec((nch, S5_T, 128), lambda g, hf, b: (b, 0, g)),
                  pl.BlockSpec((None, cw, half * 128), lambda g, hf, b: (g, 0, hf))]
                 + [pl.BlockSpec((nch, sw), lambda g, hf, b: (b, g))] * 4
                 + [pl.BlockSpec((None, 4 * sw, half * 128), lambda g, hf, b: (g, 0, hf))],
        out_specs=pl.BlockSpec((nch, half, 128), lambda g, hf, b: (b, hf, g)),
        compiler_params=_params(("parallel", "parallel", "parallel"), 56),
        name="s5_out",
    )(h4, m8, *[t.reshape(rows, nb * sw) for t in h_parts], v8)
    y = y4.reshape(B, T, D)

    return pl.pallas_call(
        _glu_kernel,
        out_shape=_sds((B, T, D), F32),
        grid=(B, nt),
        in_specs=[_x_spec(), _x_spec(), _mod_spec(nct), _const_spec((1, D)), _const_spec((1, D)),
                  _const_spec((D, 2 * D))],
        out_specs=_x_spec(),
        compiler_params=_params(("parallel", "parallel"), 48),
        name="s5_glu",
    )(y, x, mods, norm_g.reshape(1, D), d_skip.reshape(1, D), w_glu.astype(BF16))


N_FF_CHUNKS = D_FF // FF_CHUNK


def _swiglu_rows(h, wgu_ref, wd_ref, after_chunk=None):
    acc = jnp.zeros((h.shape[0], D), F32)
    for c in range(N_FF_CHUNKS):
        lo, hi = c * FF_CHUNK, (c + 1) * FF_CHUNK
        g = jnp.dot(h, wgu_ref[:, lo:hi], preferred_element_type=F32)
        u = jnp.dot(h, wgu_ref[:, D_FF + lo:D_FF + hi], preferred_element_type=F32)
        a = ((g * jax.nn.sigmoid(g)) * u).astype(BF16)
        acc = acc + jnp.dot(a, wd_ref[lo:hi, :], preferred_element_type=F32)
        if after_chunk is not None:
            after_chunk(c)
    return acc


def _ffn_kernel(x_ref, mod_ref, g_ref, wgu_ref, wd_ref, o_ref):
    xs = [x_ref[j] for j in range(2)]
    h = jnp.concatenate([_modulate(xs[j], g_ref[...], mod_ref[j, 3:4, :], mod_ref[j, 4:5, :]) for j in range(2)],
                        axis=0).astype(BF16)
    y = _swiglu_rows(h, wgu_ref, wd_ref)
    for j in range(2):
        o_ref[j] = xs[j] + mod_ref[j, 5:6, :] * y[j * TM:(j + 1) * TM, :]


def _dense_ffn(x, mods, norm_g, w_gu, w_down, n_ctx):
    B, T, _ = x.shape
    assert B % 2 == 0
    nct = n_ctx // TM
    pair_spec = pl.BlockSpec((2, TM, D), lambda b, i: (b, i, 0))
    return pl.pallas_call(
        _ffn_kernel,
        out_shape=_sds((B, T, D), F32),
        grid=(B // 2, T // TM),
        in_specs=[pair_spec,
                  pl.BlockSpec((2, None, 6, D), lambda b, i: (b, jnp.where(i >= nct, 1, 0), 0, 0)),
                  _const_spec((1, D)), _const_spec((D, 2 * D_FF)), _const_spec((D_FF, D))],
        out_specs=pair_spec,
        compiler_params=_params(("parallel", "parallel"), 56),
        name="dense_ffn",
    )(x, mods, norm_g.reshape(1, D), w_gu.astype(BF16), w_down.astype(BF16))


def _router_kernel(x_ref, mod_ref, g_ref, wr_ref, r_out):
    h = _modulate(x_ref[...], g_ref[...], mod_ref[3:4, :], mod_ref[4:5, :])
    logits = jnp.dot(h, wr_ref[...], preferred_element_type=F32, precision=HIGHEST)
    lane = lax.broadcasted_iota(jnp.int32, logits.shape, 1)
    neg = jnp.float32(-jnp.inf)
    logits = jnp.where(lane < N_EXPERTS, logits, neg)
    v1 = jnp.max(logits, axis=-1, keepdims=True)
    i1 = jnp.min(jnp.where(logits == v1, lane, 128), axis=-1, keepdims=True)
    rest = jnp.where(lane == i1, neg, logits)
    v2 = jnp.max(rest, axis=-1, keepdims=True)
    i2 = jnp.min(jnp.where(rest == v2, lane, 128), axis=-1, keepdims=True)
    e = jnp.exp(v2 - v1)
    w1 = 1.0 / (1.0 + e)
    w2 = e / (1.0 + e)
    out_lane = lax.broadcasted_iota(jnp.int32, (TM, 8), 1)
    r = jnp.where(out_lane == 0, i1.astype(F32), 0.0)
    r = jnp.where(out_lane == 1, i2.astype(F32), r)
    r = jnp.where(out_lane == 2, w1, r)
    r = jnp.where(out_lane == 3, w2, r)
    r_out[...] = r


RUN_ALIGN = 8
RUN_BITS = tuple(1 << b for b in range(TM.bit_length() - 1, RUN_ALIGN.bit_length() - 2, -1))
LOCAL_ROWS = 2 * TM + 64


def _run_copies(run_ref, local, remote, sem, *, to_remote, wait):
    for e in range(N_EXPERTS):
        l0, n, g0 = run_ref[0, 0, e], run_ref[0, 1, e], run_ref[0, 2, e]
        done = jnp.int32(0)
        for bit in RUN_BITS:
            take = (n & bit) != 0

            @pl.when(take)
            def _(done=done, bit=bit):
                loc = local.at[pl.ds(pl.multiple_of(l0 + done, RUN_ALIGN), bit)]
                rem = remote.at[pl.ds(pl.multiple_of(g0 + done, RUN_ALIGN), bit)]
                cp = pltpu.make_async_copy(loc, rem, sem) if to_remote else pltpu.make_async_copy(rem, loc, sem)
                if wait:
                    cp.wait()
                else:
                    cp.start()

            done = done + (n & bit)


def _block_onehot(lp, shape, axis):
    return jnp.where(lax.broadcasted_iota(jnp.int32, shape, axis) == lp, 1.0, 0.0).astype(BF16)


def _dispatch_kernel(run_ref, x_ref, mod_ref, g_ref, lp_ref, init_hbm, xs_hbm, sbuf, sem):
    del init_hbm
    h = _modulate(x_ref[...], g_ref[...], mod_ref[3:4, :], mod_ref[4:5, :]).astype(BF16)
    lp = lp_ref[...]
    place = _block_onehot(lp[0:1, :], (LOCAL_ROWS, TM), 0) + _block_onehot(lp[1:2, :], (LOCAL_ROWS, TM), 0)
    sbuf[...] = jnp.dot(place, h, preferred_element_type=F32)
    _run_copies(run_ref, sbuf, xs_hbm, sem, to_remote=True, wait=False)
    _run_copies(run_ref, sbuf, xs_hbm, sem, to_remote=True, wait=True)


def _expert_kernel(meta_ref, xs_ref, wgu_ref, wd_ref, y_ref):
    t = pl.program_id(0)
    n_used = meta_ref[pl.num_programs(0)]

    @pl.when(t < n_used)
    def _():
        y_ref[...] = _swiglu_rows(xs_ref[...].astype(BF16), wgu_ref, wd_ref)

    @pl.when(t >= n_used)
    def _():
        y_ref[...] = jnp.zeros_like(y_ref)


def _combine_kernel(run_ref, y_hbm, x_ref, mod_ref, r_ref, lp_ref, fin_ref, o_ref, ybuf, sem, *, final):
    ybuf[2 * TM:LOCAL_ROWS, :] = jnp.zeros((LOCAL_ROWS - 2 * TM, D), F32)
    _run_copies(run_ref, ybuf, y_hbm, sem, to_remote=False, wait=False)
    _run_copies(run_ref, ybuf, y_hbm, sem, to_remote=False, wait=True)
    ys = ybuf[...].astype(BF16)
    lp = lp_ref[...]
    y0 = jnp.dot(_block_onehot(lp[:, 0:1], (TM, LOCAL_ROWS), 1), ys, preferred_element_type=F32)
    y1 = jnp.dot(_block_onehot(lp[:, 1:2], (TM, LOCAL_ROWS), 1), ys, preferred_element_type=F32)
    r = r_ref[...]
    x = x_ref[...] + mod_ref[5:6, :] * (r[:, 2:3] * y0 + r[:, 3:4] * y1)
    if final:
        x = (x * lax.rsqrt(jnp.mean(x * x, axis=-1, keepdims=True) + EPS)) * fin_ref[...]
    o_ref[...] = x


def _moe_ffn(x, mods, norm_g, w_router, w_gu, w_down, n_ctx, final_g=None):
    B, T, _ = x.shape
    nt = T // TM
    nct = n_ctx // TM
    n_tok = B * T
    nblk = B * nt
    wr = jnp.zeros((D, 128), F32).at[:, :N_EXPERTS].set(w_router)
    route = pl.pallas_call(
        _router_kernel,
        out_shape=_sds((B, T, 8), F32),
        grid=(B, nt),
        in_specs=[_x_spec(), _mod_spec(nct), _const_spec((1, D)), _const_spec((D, 128))],
        out_specs=pl.BlockSpec((None, TM, 8), lambda b, i: (b, i, 0)),
        compiler_params=_params(("parallel", "parallel"), 32),
        name="router",
    )(x, mods, norm_g.reshape(1, D), wr)

    ids = jnp.arange(N_EXPERTS, dtype=jnp.int32)
    eid = route.reshape(nblk, TM, 8)[:, :, :2].astype(jnp.int32).reshape(nblk, 2 * TM)
    onehot = (eid[:, :, None] == ids).astype(jnp.int32)
    cnt = jnp.sum(onehot, axis=1)
    cnt = ((cnt + RUN_ALIGN - 1) // RUN_ALIGN) * RUN_ALIGN
    l0 = jnp.cumsum(cnt, axis=1) - cnt
    rank = jnp.sum((jnp.cumsum(onehot, axis=1) - onehot) * onehot, axis=2)
    lp = jnp.sum(onehot * l0[:, None, :], axis=2) + rank
    counts = jnp.sum(cnt, axis=0)
    padded = ((counts + TME - 1) // TME) * TME
    ends = jnp.cumsum(padded)
    g0 = (ends - padded)[None, :] + jnp.cumsum(cnt, axis=0) - cnt
    runs = jnp.stack([l0, cnt, g0], axis=1)
    lp_cols = lp.reshape(nblk, TM, 2)
    lp_rows = lp_cols.transpose(0, 2, 1)
    n_tiles = -(-(2 * n_tok + nblk * N_EXPERTS * (RUN_ALIGN - 1) + N_EXPERTS * (TME - 1)) // TME)
    n_rows = n_tiles * TME
    tile_row = jnp.arange(n_tiles, dtype=jnp.int32)[:, None] * TME
    tile_exp = jnp.minimum(jnp.sum((ends[None, :] <= tile_row).astype(jnp.int32), axis=1), N_EXPERTS - 1)
    meta = jnp.concatenate([tile_exp, (ends[-1:] // TME).astype(jnp.int32)])

    run_spec = lambda off: pl.BlockSpec((1, 3, N_EXPERTS), lambda b, i: (b * nt + i + off, 0, 0),
                                        memory_space=pltpu.SMEM)
    xs = pl.pallas_call(
        _dispatch_kernel,
        out_shape=_sds((n_rows, D), F32),
        grid=(B, nt),
        in_specs=[run_spec(0), _x_spec(), _mod_spec(nct), _const_spec((1, D)),
                  pl.BlockSpec((None, 2, TM), lambda b, i: (b * nt + i, 0, 0)),
                  pl.BlockSpec(memory_space=pl.ANY)],
        out_specs=pl.BlockSpec(memory_space=pl.ANY),
        scratch_shapes=[pltpu.VMEM((LOCAL_ROWS, D), F32), pltpu.SemaphoreType.DMA(())],
        input_output_aliases={5: 0},
        compiler_params=_params(("arbitrary", "arbitrary"), 40),
        name="moe_dispatch",
    )(runs, x, mods, norm_g.reshape(1, D), lp_rows, jnp.zeros((n_rows, D), F32))

    y = pl.pallas_call(
        _expert_kernel,
        out_shape=_sds((n_rows, D), F32),
        grid_spec=pltpu.PrefetchScalarGridSpec(
            num_scalar_prefetch=1,
            grid=(n_tiles,),
            in_specs=[pl.BlockSpec((TME, D), lambda t, meta: (t, 0)),
                      pl.BlockSpec((None, D, 2 * D_FF), lambda t, meta: (meta[t], 0, 0)),
                      pl.BlockSpec((None, D_FF, D), lambda t, meta: (meta[t], 0, 0))],
            out_specs=pl.BlockSpec((TME, D), lambda t, meta: (t, 0))),
        compiler_params=_params(("arbitrary",), 56),
        name="experts",
    )(meta, xs, w_gu.astype(BF16), w_down.astype(BF16))

    final = final_g is not None
    off = nct if final else 0
    tok_spec = lambda last: pl.BlockSpec((None, TM, last), lambda b, i: (b, i + off, 0))
    return pl.pallas_call(
        functools.partial(_combine_kernel, final=final),
        out_shape=_sds((B, T - off * TM, D), F32),
        grid=(B, nt - off),
        in_specs=[run_spec(off), pl.BlockSpec(memory_space=pl.ANY), tok_spec(D),
                  pl.BlockSpec((None, None, 6, D), lambda b, i: (b, jnp.where(i + off >= nct, 1, 0), 0, 0)),
                  tok_spec(8),
                  pl.BlockSpec((None, TM, 2), lambda b, i: (b * nt + i + off, 0, 0)),
                  _const_spec((1, D))],
        out_specs=_x_spec(),
        scratch_shapes=[pltpu.VMEM((LOCAL_ROWS, D), F32), pltpu.SemaphoreType.DMA(())],
        compiler_params=_params(("arbitrary", "arbitrary"), 40),
        name="moe_combine",
    )(runs, y, x, mods, route, lp_cols, (final_g if final else norm_g).reshape(1, D))


def kernel(x, c, ctx, c_ctx, l0_ada_w, l0_ada_b, l0_norm_mix, l0_norm_ffn, l0_conv_w_in, l0_conv_w, l0_conv_w_out, l0_ffn_w_gu, l0_ffn_w_down, l1_ada_w, l1_ada_b, l1_norm_mix, l1_norm_ffn, l1_attn_w_qkv, l1_attn_lam, l1_attn_subln, l1_attn_w_o, l1_moe_router, l1_moe_w_gu, l1_moe_w_down, l2_ada_w, l2_ada_b, l2_norm_mix, l2_norm_ffn, l2_ssm_a_re, l2_ssm_a_im, l2_ssm_log_dt, l2_ssm_b_re, l2_ssm_b_im, l2_ssm_c_re, l2_ssm_c_im, l2_ssm_d, l2_ssm_w_glu, l2_ffn_w_gu, l2_ffn_w_down, l3_ada_w, l3_ada_b, l3_norm_mix, l3_norm_ffn, l3_conv_w_in, l3_conv_w, l3_conv_w_out, l3_moe_router, l3_moe_w_gu, l3_moe_w_down, final_norm):
    B, n_lat, _ = x.shape
    n_ctx = ctx.shape[1]
    T = n_ctx + n_lat
    assert n_ctx % TM == 0 and n_lat % TM == 0 and n_lat % GRID_W == 0 and B < 8
    xs = jnp.concatenate([ctx, x], axis=1)
    cond8 = jnp.concatenate([c, c_ctx[None], jnp.zeros((8 - B - 1, D), F32)], axis=0)

    def mods_of(w, b):
        m = _ada(cond8, w, b).reshape(8, 6, D)
        return jnp.stack([jnp.broadcast_to(m[B], (B, 6, D)), m[:B]], axis=1)

    mods = mods_of(l0_ada_w, l0_ada_b)
    xs = _conv_mixer(xs, mods, l0_norm_mix, l0_conv_w_in, l0_conv_w, l0_conv_w_out, n_ctx)
    xs = _dense_ffn(xs, mods, l0_norm_ffn, l0_ffn_w_gu, l0_ffn_w_down, n_ctx)

    mods = mods_of(l1_ada_w, l1_ada_b)
    xs = _attn_mixer(xs, mods, l1_norm_mix, l1_attn_w_qkv, l1_attn_lam, l1_attn_subln, l1_attn_w_o, n_ctx,
                     0.8 - 0.6 * math.exp(-0.3 * 1))
    xs = _moe_ffn(xs, mods, l1_norm_ffn, l1_moe_router, l1_moe_w_gu, l1_moe_w_down, n_ctx)

    mods = mods_of(l2_ada_w, l2_ada_b)
    xs = _s5_mixer(xs, mods, l2_norm_mix,
                   (l2_ssm_a_re, l2_ssm_a_im, l2_ssm_log_dt, l2_ssm_b_re, l2_ssm_b_im, l2_ssm_c_re, l2_ssm_c_im,
                    l2_ssm_d, l2_ssm_w_glu), n_ctx)
    xs = _dense_ffn(xs, mods, l2_norm_ffn, l2_ffn_w_gu, l2_ffn_w_down, n_ctx)

    mods = mods_of(l3_ada_w, l3_ada_b)
    xs = _conv_mixer(xs, mods, l3_norm_mix, l3_conv_w_in, l3_conv_w, l3_conv_w_out, n_ctx)
    return _moe_ffn(xs, mods, l3_norm_ffn, l3_moe_router, l3_moe_w_gu, l3_moe_w_down, n_ctx, final_g=final_norm)
```

```python
import functools
import math

import jax
import jax.numpy as jnp
from jax import lax
from jax.experimental import pallas as pl
from jax.experimental.pallas import tpu as pltpu

F32 = jnp.float32
BF16 = jnp.bfloat16
HIGHEST = lax.Precision.HIGHEST

D = 1024
GRID_W = 64
N_HEADS = 8
HEAD_DIM = 64
V_DIM = 128
ROPE_BASE = 10000.0
GROUP_SIZE = 16
N_GROUPS = 64
STATE_DIM = 64
D_FF = 2816
N_EXPERTS = 8
EPS = 1e-6

TM = 256
FF_CHUNK = 256
TK = 256
S5_T = 16
TME = 512
MIB = 1 << 20


def _sds(shape, dtype):
    return jax.ShapeDtypeStruct(shape, dtype)


def _params(sem, vmem_mib):
    return pltpu.CompilerParams(dimension_semantics=sem, vmem_limit_bytes=vmem_mib * MIB)


def _modulate(x, g, shift, scale):
    y = x * lax.rsqrt(jnp.mean(x * x, axis=-1, keepdims=True) + EPS)
    return (y * g) * (1.0 + scale) + shift


def _x_spec():
    return pl.BlockSpec((None, TM, D), lambda b, i: (b, i, 0))


def _mod_spec(n_ctx_tiles):
    return pl.BlockSpec((None, None, 6, D), lambda b, i: (b, jnp.where(i >= n_ctx_tiles, 1, 0), 0, 0))


def _const_spec(shape):
    nd = len(shape)
    return pl.BlockSpec(shape, lambda b, i: (0,) * nd)


def _ada_kernel(c_ref, w_ref, b_ref, o_ref):
    c = c_ref[...]
    s = c * jax.nn.sigmoid(c)
    o_ref[...] = jnp.dot(s, w_ref[...], preferred_element_type=F32, precision=HIGHEST) + b_ref[...]


def _ada(cond8, w, b):
    nb = 4
    bn = w.shape[1] // nb
    return pl.pallas_call(
        _ada_kernel,
        out_shape=_sds((8, w.shape[1]), F32),
        grid=(nb,),
        in_specs=[pl.BlockSpec((8, D), lambda j: (0, 0)),
                  pl.BlockSpec((D, bn), lambda j: (0, j)),
                  pl.BlockSpec((1, bn), lambda j: (0, j))],
        out_specs=pl.BlockSpec((8, bn), lambda j: (0, j)),
        compiler_params=_params(("parallel",), 40),
        name="ada",
    )(cond8, w, b.reshape(1, -1))


def _conv_in_kernel(x_ref, mod_ref, g_ref, w_ref, b_out, z_out):
    h = _modulate(x_ref[...], g_ref[...], mod_ref[0:1, :], mod_ref[1:2, :]).astype(BF16)
    b_out[...] = jnp.dot(h, w_ref[:, 0:D], preferred_element_type=F32).astype(BF16)
    c_gate = jnp.dot(h, w_ref[:, D:2 * D], preferred_element_type=F32)
    v = jnp.dot(h, w_ref[:, 2 * D:3 * D], preferred_element_type=F32)
    z_out[...] = c_gate * v


def _conv_out_kernel(z_ref, zp_ref, zn_ref, b_ref, cw_ref, w_ref, x_ref, mod_ref, o_ref, *, n_ctx, n_tot):
    i = pl.program_id(1)
    z = z_ref[...]
    row = lax.broadcasted_iota(jnp.int32, (TM, 1), 0)
    grow = row + i * TM
    prev_row = zp_ref[7:8, :]
    next_row = zn_ref[0:1, :]
    zm1 = jnp.where(row == 0, prev_row, pltpu.roll(z, 1, axis=0))
    zp1 = jnp.where(row == TM - 1, next_row, pltpu.roll(z, TM - 1, axis=0))
    zm1 = jnp.where((grow == 0) | (grow == n_ctx), 0.0, zm1)
    zp1 = jnp.where((grow == n_ctx - 1) | (grow == n_tot - 1), 0.0, zp1)
    y = cw_ref[0:1, :] * zm1 + cw_ref[1:2, :] * z + cw_ref[2:3, :] * zp1
    a = (b_ref[...].astype(F32) * y).astype(BF16)
    o_ref[...] = x_ref[...] + mod_ref[2:3, :] * jnp.dot(a, w_ref[...], preferred_element_type=F32)


def _conv_mixer(x, mods, norm_g, w_in, conv_w, w_out, n_ctx):
    B, T, _ = x.shape
    nt = T // TM
    nct = n_ctx // TM
    b_gate, z = pl.pallas_call(
        _conv_in_kernel,
        out_shape=(_sds((B, T, D), BF16), _sds((B, T, D), F32)),
        grid=(B, nt),
        in_specs=[_x_spec(), _mod_spec(nct), _const_spec((1, D)), _const_spec((D, 3 * D))],
        out_specs=(_x_spec(), _x_spec()),
        compiler_params=_params(("parallel", "parallel"), 48),
        name="conv_in",
    )(x, mods, norm_g.reshape(1, D), w_in.astype(BF16))
    r8 = TM // 8
    return pl.pallas_call(
        functools.partial(_conv_out_kernel, n_ctx=n_ctx, n_tot=T),
        out_shape=_sds((B, T, D), F32),
        grid=(B, nt),
        in_specs=[_x_spec(),
                  pl.BlockSpec((None, 8, D), lambda b, i: (b, jnp.maximum(i * r8 - 1, 0), 0)),
                  pl.BlockSpec((None, 8, D), lambda b, i: (b, jnp.minimum((i + 1) * r8, T // 8 - 1), 0)),
                  _x_spec(), _const_spec((3, D)), _const_spec((D, D)), _x_spec(), _mod_spec(nct)],
        out_specs=_x_spec(),
        compiler_params=_params(("parallel", "parallel"), 48),
        name="conv_out",
    )(z, z, z, b_gate, conv_w, w_out.astype(BF16), x, mods)


def _qkv_kernel(x_ref, mod_ref, g_ref, w_ref, cos_ref, slo_ref, shi_ref, qt_out, k_out, vt_out):
    h = _modulate(x_ref[...], g_ref[...], mod_ref[0:1, :], mod_ref[1:2, :]).astype(BF16)
    cos = cos_ref[...]
    s_lo = slo_ref[...]
    s_hi = shi_ref[...]

    def rope(t):
        return t * cos + pltpu.roll(t, 128 - 16, axis=1) * s_lo + pltpu.roll(t, 16, axis=1) * s_hi

    q_scale = (HEAD_DIM ** -0.5) * math.log2(math.e)
    for hd in range(N_HEADS):
        lo, hi = hd * 128, (hd + 1) * 128
        q = jnp.dot(h, w_ref[:, lo:hi], preferred_element_type=F32) * q_scale
        qt_out[hd] = rope(q).T.astype(BF16)
        k = jnp.dot(h, w_ref[:, D + lo:D + hi], preferred_element_type=F32)
        k_out[:, lo:hi] = rope(k).astype(BF16)
        v = jnp.dot(h, w_ref[:, 2 * D + lo:2 * D + hi], preferred_element_type=F32)
        vt_out[hd] = v.astype(BF16).astype(F32).T.astype(BF16)


def _attn_kernel(qt_ref, k_ref, vt_ref, lam_ref, sub_ref, o_ref, s_buf, acc_ref, *,
                 n_ctx_tiles, n_lat_chunks, lam_init):
    i = pl.program_id(2)
    qt = qt_ref[...]
    row = lax.broadcasted_iota(jnp.int32, qt.shape, 0)
    zero = jnp.zeros_like(qt)
    q_cat = jnp.concatenate([jnp.where(row < HEAD_DIM, qt, zero), jnp.where(row >= HEAD_DIM, qt, zero)], axis=1)

    def scores(first_tile, n_tiles):
        start = first_tile * TM
        if not isinstance(first_tile, int):
            start = pl.multiple_of(start, TM)
        s = jnp.dot(k_ref[pl.ds(start, n_tiles * TM), :], q_cat, preferred_element_type=F32)
        return s, jnp.max(s, axis=0, keepdims=True)

    def absorb(s, s_max, first_tile, n_tiles, m, l):
        m_new = jnp.maximum(m, s_max)
        p = jnp.exp2(s - m_new)
        alpha = jnp.exp2(m - m_new)
        l_new = alpha * l + jnp.sum(p, axis=0, keepdims=True)
        pb = p.astype(BF16)
        pv = jnp.dot(vt_ref[first_tile], pb[0:TM, :], preferred_element_type=F32)
        for c in range(1, n_tiles):
            pv = pv + jnp.dot(vt_ref[first_tile + c], pb[c * TM:(c + 1) * TM, :], preferred_element_type=F32)
        acc_ref[...] = alpha * acc_ref[...] + pv
        return m_new, l_new

    def finish(l):
        acc = acc_ref[...]
        lv = lam_ref[...]
        lam = (jnp.exp(jnp.sum(lv[0:1, :] * lv[1:2, :], axis=-1, keepdims=True))
               - jnp.exp(jnp.sum(lv[2:3, :] * lv[3:4, :], axis=-1, keepdims=True)) + lam_init)
        o = (acc[:, 0:TM] / l[:, 0:TM] - lam * (acc[:, TM:2 * TM] / l[:, TM:2 * TM])).T
        o = o * lax.rsqrt(jnp.mean(o * o, axis=-1, keepdims=True) + EPS)
        o_ref[...] = ((o * sub_ref[...]) * (1.0 - lam_init)).astype(BF16)

    acc_ref[...] = jnp.zeros_like(acc_ref)
    s_ctx, max_ctx = scores(0, n_ctx_tiles)
    s_buf[0], max_first = scores(n_ctx_tiles, 2)
    m_ctx, l_ctx = absorb(s_ctx, max_ctx, 0, n_ctx_tiles,
                          jnp.full((1, 2 * TM), -1e30, F32), jnp.zeros((1, 2 * TM), F32))

    @pl.when(i < n_ctx_tiles)
    def _():
        finish(l_ctx)

    @pl.when(i >= n_ctx_tiles)
    def _():
        m, l, maxes = m_ctx, l_ctx, [max_first, None]
        for c in range(n_lat_chunks):
            if c + 1 < n_lat_chunks:
                s_buf[(c + 1) % 2], maxes[(c + 1) % 2] = scores(n_ctx_tiles + 2 * (c + 1), 2)
            m, l = absorb(s_buf[c % 2], maxes[c % 2], n_ctx_tiles + 2 * c, 2, m, l)
        finish(l)


def _out_proj_kernel(a_ref, w_ref, x_ref, mod_ref, o_ref):
    o_ref[...] = x_ref[...] + mod_ref[2:3, :] * jnp.dot(a_ref[...], w_ref[...], preferred_element_type=F32)


def _rope_tables(n_ctx, n_lat):
    t = jnp.arange(n_lat, dtype=jnp.int32)
    row = (t // GRID_W).astype(F32)
    col = (t % GRID_W).astype(F32)
    n_freq = HEAD_DIM // 4
    inv = ROPE_BASE ** (-jnp.arange(n_freq, dtype=F32) / n_freq)
    ang_r = row[:, None] * inv
    ang_c = col[:, None] * inv
    zero = jnp.zeros_like(ang_r)
    cos64 = jnp.concatenate([jnp.cos(ang_r), jnp.cos(ang_r), jnp.cos(ang_c), jnp.cos(ang_c)], axis=-1)
    slo64 = jnp.concatenate([-jnp.sin(ang_r), zero, -jnp.sin(ang_c), zero], axis=-1)
    shi64 = jnp.concatenate([zero, jnp.sin(ang_r), zero, jnp.sin(ang_c)], axis=-1)

    def full(tab, ctx_val):
        tab = jnp.concatenate([tab, tab], axis=-1)
        return jnp.concatenate([jnp.full((n_ctx, 128), ctx_val, F32), tab], axis=0)

    return full(cos64, 1.0), full(slo64, 0.0), full(shi64, 0.0)


def _attn_mixer(x, mods, norm_g, w_qkv, lam_vecs, subln, w_o, n_ctx, lam_init):
    B, T, _ = x.shape
    nt = T // TM
    nct = n_ctx // TM
    cos, s_lo, s_hi = _rope_tables(n_ctx, T - n_ctx)
    tab_spec = pl.BlockSpec((TM, 128), lambda b, i: (i, 0))
    assert (T - n_ctx) % (4 * TM) == 0
    t_spec = pl.BlockSpec((None, N_HEADS, None, 128, TM), lambda b, i: (b, 0, i, 0, 0))
    t_shape = _sds((B, N_HEADS, nt, 128, TM), BF16)
    qt, k, vt = pl.pallas_call(
        _qkv_kernel,
        out_shape=(t_shape, _sds((B, T, D), BF16), t_shape),
        grid=(B, nt),
        in_specs=[_x_spec(), _mod_spec(nct), _const_spec((1, D)), _const_spec((D, 3 * D)),
                  tab_spec, tab_spec, tab_spec],
        out_specs=(t_spec, _x_spec(), t_spec),
        compiler_params=_params(("parallel", "parallel"), 48),
        name="qkv",
    )(x, mods, norm_g.reshape(1, D), w_qkv.astype(BF16), cos, s_lo, s_hi)

    attn = pl.pallas_call(
        functools.partial(_attn_kernel, n_ctx_tiles=nct, n_lat_chunks=(T - n_ctx) // (2 * TM), lam_init=lam_init),
        out_shape=_sds((B, T, D), BF16),
        grid=(B, N_HEADS, nt),
        in_specs=[pl.BlockSpec((None, None, None, 128, TM), lambda b, h, i: (b, h, i, 0, 0)),
                  pl.BlockSpec((None, T, 128), lambda b, h, i: (b, 0, h)),
                  pl.BlockSpec((None, None, nt, 128, TM), lambda b, h, i: (b, h, 0, 0, 0)),
                  pl.BlockSpec((4, HEAD_DIM), lambda b, h, i: (0, 0)),
                  pl.BlockSpec((1, V_DIM), lambda b, h, i: (0, 0))],
        out_specs=pl.BlockSpec((None, TM, 128), lambda b, h, i: (b, i, h)),
        scratch_shapes=[pltpu.VMEM((2, 2 * TM, 2 * TM), F32), pltpu.VMEM((V_DIM, 2 * TM), F32)],
        compiler_params=_params(("parallel", "parallel", "parallel"), 48),
        name="diff_attn",
    )(qt, k, vt, lam_vecs, subln.reshape(1, V_DIM))

    return pl.pallas_call(
        _out_proj_kernel,
        out_shape=_sds((B, T, D), F32),
        grid=(B, nt),
        in_specs=[_x_spec(), _const_spec((D, D)), _x_spec(), _mod_spec(nct)],
        out_specs=_x_spec(),
        compiler_params=_params(("parallel", "parallel"), 48),
        name="attn_out",
    )(attn, w_o.astype(BF16), x, mods)


def _mod_only_kernel(x_ref, mod_ref, g_ref, h_out):
    h_out[...] = _modulate(x_ref[...], g_ref[...], mod_ref[0:1, :], mod_ref[1:2, :])


def _s5_chunk_rows(h_ref):
    return jnp.concatenate([h_ref[:, s, :] for s in range(S5_T)], axis=-1).astype(BF16)


def _s5_state_kernel(h_ref, w_ref, sfr, sfi, sbr, sbi):
    s = jnp.dot(_s5_chunk_rows(h_ref), w_ref[...], preferred_element_type=F32)
    sfr[...] = s[:, 0:512]
    sfi[...] = s[:, 512:1024]
    sbr[...] = s[:, 1024:1536]
    sbi[...] = s[:, 1536:2048]


def _s5_scan_kernel(sfr, sfi, sbr, sbi, afr, afi, abr, abi, hfr, hfi, hbr, hbi, *, n_ctx_chunks, n_chunks):
    ar, ai = afr[...], afi[...]

    def fwd(n, carry):
        hr, hi = carry
        hfr[n] = hr
        hfi[n] = hi
        sr, si = sfr[n], sfi[n]
        return ar * hr - ai * hi + sr, ar * hi + ai * hr + si

    zero = jnp.zeros(ar.shape, F32)
    lax.fori_loop(0, n_chunks, fwd, (zero, zero))

    br, bi = abr[...], abi[...]

    def bwd(t, carry, top):
        n = top - t
        hr, hi = carry
        hbr[n] = hr
        hbi[n] = hi
        sr, si = sbr[n], sbi[n]
        return br * hr - bi * hi + sr, br * hi + bi * hr + si

    carry = lax.fori_loop(0, n_ctx_chunks, functools.partial(bwd, top=n_ctx_chunks - 1), (zero, zero))
    lax.fori_loop(0, n_chunks - n_ctx_chunks, functools.partial(bwd, top=n_chunks - 1), carry)


def _s5_out_kernel(h_ref, m_ref, hfr, hfi, hbr, hbi, v_ref, y_ref):
    state = jnp.concatenate([hfr[...], hfi[...], hbr[...], hbi[...]], axis=-1).astype(BF16)
    y = (jnp.dot(_s5_chunk_rows(h_ref), m_ref[...], preferred_element_type=F32)
         + jnp.dot(state, v_ref[...], preferred_element_type=F32))
    for t in range(S5_T // 2):
        y_ref[:, t, :] = y[:, t * 128:(t + 1) * 128]


def _glu_kernel(y_ref, x_ref, mod_ref, g_ref, d_ref, w_ref, o_ref):
    x = x_ref[...]
    hl = _modulate(x, g_ref[...], mod_ref[0:1, :], mod_ref[1:2, :])
    y = y_ref[...] + d_ref[...] * hl
    a = jax.nn.gelu(y).astype(BF16)
    val = jnp.dot(a, w_ref[:, 0:D], preferred_element_type=F32)
    gate = jnp.dot(a, w_ref[:, D:2 * D], preferred_element_type=F32)
    o_ref[...] = x + mod_ref[2:3, :] * (val * jax.nn.sigmoid(gate))


def _s5_matrices(a_re, a_im, log_dt, b_re, b_im, c_re, c_im):
    T = S5_T
    hp = functools.partial(jnp.einsum, precision=HIGHEST)
    pw_re, pw_im, bb_re, bb_im = [], [], [], []
    for d in range(2):
        ar, ai = a_re[d].astype(F32), a_im[d].astype(F32)
        dt = jnp.exp(log_dt[d].astype(F32))[:, None]
        mag = jnp.exp(dt * ar)
        ab_re = mag * jnp.cos(dt * ai)
        ab_im = mag * jnp.sin(dt * ai)
        den = ar * ar + ai * ai
        nr = ab_re - 1.0
        co_re = (nr * ar + ab_im * ai) / den
        co_im = (ab_im * ar - nr * ai) / den
        br, bi = b_re[d].astype(F32), b_im[d].astype(F32)
        bb_re.append(co_re[..., None] * br - co_im[..., None] * bi)
        bb_im.append(co_re[..., None] * bi + co_im[..., None] * br)
        pr = [jnp.ones_like(ab_re)]
        pi = [jnp.zeros_like(ab_re)]
        for _ in range(T):
            pr.append(pr[-1] * ab_re - pi[-1] * ab_im)
            pi.append(pr[-2] * ab_im + pi[-1] * ab_re)
        pw_re.append(jnp.stack(pr))
        pw_im.append(jnp.stack(pi))

    def lag_kernels(d):
        mr = pw_re[d][:T, :, :, None] * bb_re[d][None] - pw_im[d][:T, :, :, None] * bb_im[d][None]
        mi = pw_re[d][:T, :, :, None] * bb_im[d][None] + pw_im[d][:T, :, :, None] * bb_re[d][None]
        return hp('gop,kgpi->kgoi', c_re[d].astype(F32), mr) - hp('gop,kgpi->kgoi', c_im[d].astype(F32), mi)

    kf, kb = lag_kernels(0), lag_kernels(1)
    lag0 = (kf[0] + kb[0])[None]
    kfull = jnp.concatenate([kb[:0:-1], lag0, kf[1:]], axis=0)
    tt = jnp.arange(T)
    idx = tt[None, :] - tt[:, None] + (T - 1)
    mt = kfull[idx]
    mt = mt.transpose(2, 0, 4, 1, 3).reshape(N_GROUPS, T * GROUP_SIZE, T * GROUP_SIZE)

    def end_state(d, powers):
        pr, pi = pw_re[d][powers], pw_im[d][powers]
        wr = pr[..., None] * bb_re[d][None] - pi[..., None] * bb_im[d][None]
        wi = pr[..., None] * bb_im[d][None] + pi[..., None] * bb_re[d][None]
        f = lambda w: w.transpose(1, 0, 3, 2).reshape(N_GROUPS, T * GROUP_SIZE, STATE_DIM)
        return f(wr), f(wi)

    def read_out(d, powers):
        pr, pi = pw_re[d][powers], pw_im[d][powers]
        cr, ci = c_re[d].astype(F32), c_im[d].astype(F32)
        qr = cr[None] * pr[:, :, None, :] - ci[None] * pi[:, :, None, :]
        qi = cr[None] * pi[:, :, None, :] + ci[None] * pr[:, :, None, :]
        f = lambda q: q.transpose(1, 3, 0, 2).reshape(N_GROUPS, STATE_DIM, T * GROUP_SIZE)
        return f(qr), f(-qi)

    wfr, wfi = end_state(0, T - 1 - tt)
    wbr, wbi = end_state(1, tt)
    vfr, vfi = read_out(0, tt + 1)
    vbr, vbi = read_out(1, T - tt)

    gsz = T * GROUP_SIZE
    g8 = jnp.arange(8, dtype=jnp.int32)[:, None]
    k = jnp.arange(gsz, dtype=jnp.int32)[None, :]
    chunk_col = (k // GROUP_SIZE) * 128 + g8 * GROUP_SIZE + k % GROUP_SIZE
    state_col = (k // STATE_DIM) * (8 * STATE_DIM) + g8 * STATE_DIM + k % STATE_DIM
    lanes = jnp.arange(8 * gsz, dtype=jnp.int32)[None, None, :]
    to_chunk = (chunk_col[:, :, None] == lanes).astype(BF16)
    to_state = (state_col[:, :, None] == lanes).astype(BF16)
    w_cat = jnp.stack([wfr, wfi, wbr, wbi], axis=2)
    v_cat = jnp.stack([vfr, vfi, vbr, vbi], axis=1)
    m8 = _spread(mt, to_chunk, T, GROUP_SIZE)
    w8 = _spread(w_cat.reshape(N_GROUPS, gsz, gsz), to_state, T, GROUP_SIZE)
    v8 = _spread(v_cat.reshape(N_GROUPS, gsz, gsz), to_chunk, 4, STATE_DIM)
    a_pow = [t[T].reshape(N_GROUPS // 2, 2 * STATE_DIM) for t in (pw_re[0], pw_im[0], pw_re[1], pw_im[1])]
    return m8, w8, v8, a_pow


def _spread_kernel(a_ref, c_ref, o_ref):
    e = jnp.dot(a_ref[...], c_ref[...], preferred_element_type=F32)
    o_ref[...] = e.reshape(o_ref.shape).astype(BF16)


def _spread(a, col_of, r1, r2):
    nb = N_GROUPS // 8
    kdim, n = col_of.shape[1], col_of.shape[2]
    out = pl.pallas_call(
        _spread_kernel,
        out_shape=_sds((nb, r1, 8, r2, n), BF16),
        grid=(nb, 8),
        in_specs=[pl.BlockSpec((None, None, r1 * r2, kdim), lambda g, j: (g, j, 0, 0)),
                  pl.BlockSpec((None, kdim, n), lambda g, j: (j, 0, 0))],
        out_specs=pl.BlockSpec((None, r1, None, r2, n), lambda g, j: (g, 0, j, 0, 0)),
        compiler_params=_params(("parallel", "parallel"), 32),
        name="s5_spread",
    )(a.astype(BF16).reshape(nb, 8, r1 * r2, kdim), col_of)
    return out.reshape(nb, r1 * 8 * r2, n)


def _s5_mixer(x, mods, norm_g, ssm, n_ctx):
    a_re, a_im, log_dt, b_re, b_im, c_re, c_im, d_skip, w_glu = ssm
    B, T, _ = x.shape
    nt = T // TM
    nct = n_ctx // TM
    nch = T // S5_T
    nb = N_GROUPS // 8
    sw = 8 * STATE_DIM
    cw = S5_T * 128
    m8, w8, v8, a_pow = _s5_matrices(a_re, a_im, log_dt, b_re, b_im, c_re, c_im)

    h = pl.pallas_call(
        _mod_only_kernel,
        out_shape=_sds((B, T, D), F32),
        grid=(B, nt),
        in_specs=[_x_spec(), _mod_spec(nct), _const_spec((1, D))],
        out_specs=_x_spec(),
        compiler_params=_params(("parallel", "parallel"), 32),
        name="s5_mod",
    )(x, mods, norm_g.reshape(1, D))
    h4 = h.reshape(B * nch, S5_T, D)

    rows = B * nch
    state_shape = _sds((rows, nb * sw), F32)
    s_parts = pl.pallas_call(
        _s5_state_kernel,
        out_shape=(state_shape,) * 4,
        grid=(nb, B),
        in_specs=[pl.BlockSpec((nch, S5_T, 128), lambda g, b: (b, 0, g)),
                  pl.BlockSpec((None, cw, 4 * sw), lambda g, b: (g, 0, 0))],
        out_specs=(pl.BlockSpec((nch, sw), lambda g, b: (b, g)),) * 4,
        compiler_params=_params(("parallel", "parallel"), 56),
        name="s5_state",
    )(h4, w8)

    n_pairs = N_GROUPS // 2
    npb = 8
    view = lambda t: t.reshape(B, nch, n_pairs, 128)
    scan_blk = pl.BlockSpec((None, nch, npb, 128), lambda b, j: (b, 0, j, 0))
    coef_blk = pl.BlockSpec((npb, 128), lambda b, j: (j, 0))
    h_parts = pl.pallas_call(
        functools.partial(_s5_scan_kernel, n_ctx_chunks=n_ctx // S5_T, n_chunks=nch),
        out_shape=(_sds((B, nch, n_pairs, 128), F32),) * 4,
        grid=(B, n_pairs // npb),
        in_specs=[scan_blk] * 4 + [coef_blk] * 4,
        out_specs=(scan_blk,) * 4,
        compiler_params=_params(("parallel", "parallel"), 56),
        name="s5_scan",
    )(*[view(t) for t in s_parts], *a_pow)

    half = S5_T // 2
    y4 = pl.pallas_call(
        _s5_out_kernel,
        out_shape=_sds((rows, S5_T, D), F32),
        grid=(nb, 2, B),
        in_specs=[pl.BlockSpec((nch, S5_T, 128), lambda g, hf, b: (b, 0, g)),
                  pl.BlockSpec((None, cw, half * 128), lambda g, hf, b: (g, 0, hf))]
                 + [pl.BlockSpec((nch, sw), lambda g, hf, b: (b, g))] * 4
                 + [pl.BlockSpec((None, 4 * sw, half * 128), lambda g, hf, b: (g, 0, hf))],
        out_specs=pl.BlockSpec((nch, half, 128), lambda g, hf, b: (b, hf, g)),
        compiler_params=_params(("parallel", "parallel", "parallel"), 56),
        name="s5_out",
    )(h4, m8, *[t.reshape(rows, nb * sw) for t in h_parts], v8)
    y = y4.reshape(B, T, D)

    return pl.pallas_call(
        _glu_kernel,
        out_shape=_sds((B, T, D), F32),
        grid=(B, nt),
        in_specs=[_x_spec(), _x_spec(), _mod_spec(nct), _const_spec((1, D)), _const_spec((1, D)),
                  _const_spec((D, 2 * D))],
        out_specs=_x_spec(),
        compiler_params=_params(("parallel", "parallel"), 48),
        name="s5_glu",
    )(y, x, mods, norm_g.reshape(1, D), d_skip.reshape(1, D), w_glu.astype(BF16))


N_FF_CHUNKS = D_FF // FF_CHUNK


def _swiglu_rows(h, wgu_ref, wd_ref, after_chunk=None):
    acc = jnp.zeros((h.shape[0], D), F32)
    for c in range(N_FF_CHUNKS):
        lo, hi = c * FF_CHUNK, (c + 1) * FF_CHUNK
        g = jnp.dot(h, wgu_ref[:, lo:hi], preferred_element_type=F32)
        u = jnp.dot(h, wgu_ref[:, D_FF + lo:D_FF + hi], preferred_element_type=F32)
        a = ((g * jax.nn.sigmoid(g)) * u).astype(BF16)
        acc = acc + jnp.dot(a, wd_ref[lo:hi, :], preferred_element_type=F32)
        if after_chunk is not None:
            after_chunk(c)
    return acc


def _ffn_kernel(x_ref, mod_ref, g_ref, wgu_ref, wd_ref, o_ref):
    xs = [x_ref[j] for j in range(2)]
    h = jnp.concatenate([_modulate(xs[j], g_ref[...], mod_ref[j, 3:4, :], mod_ref[j, 4:5, :]) for j in range(2)],
                        axis=0).astype(BF16)
    y = _swiglu_rows(h, wgu_ref, wd_ref)
    for j in range(2):
        o_ref[j] = xs[j] + mod_ref[j, 5:6, :] * y[j * TM:(j + 1) * TM, :]


def _dense_ffn(x, mods, norm_g, w_gu, w_down, n_ctx):
    B, T, _ = x.shape
    assert B % 2 == 0
    nct = n_ctx // TM
    pair_spec = pl.BlockSpec((2, TM, D), lambda b, i: (b, i, 0))
    return pl.pallas_call(
        _ffn_kernel,
        out_shape=_sds((B, T, D), F32),
        grid=(B // 2, T // TM),
        in_specs=[pair_spec,
                  pl.BlockSpec((2, None, 6, D), lambda b, i: (b, jnp.where(i >= nct, 1, 0), 0, 0)),
                  _const_spec((1, D)), _const_spec((D, 2 * D_FF)), _const_spec((D_FF, D))],
        out_specs=pair_spec,
        compiler_params=_params(("parallel", "parallel"), 56),
        name="dense_ffn",
    )(x, mods, norm_g.reshape(1, D), w_gu.astype(BF16), w_down.astype(BF16))


def _router_kernel(x_ref, mod_ref, g_ref, whi_ref, wlo_ref, r_out):
    h = _modulate(x_ref[...], g_ref[...], mod_ref[3:4, :], mod_ref[4:5, :])
    h_hi = h.astype(BF16)
    h_lo = (h - h_hi.astype(F32)).astype(BF16)
    logits = (jnp.dot(h_hi, whi_ref[...], preferred_element_type=F32)
              + jnp.dot(h_lo, whi_ref[...], preferred_element_type=F32)
              + jnp.dot(h_hi, wlo_ref[...], preferred_element_type=F32))
    lane = lax.broadcasted_iota(jnp.int32, logits.shape, 1)
    neg = jnp.float32(-jnp.inf)
    logits = jnp.where(lane < N_EXPERTS, logits, neg)
    v1 = jnp.max(logits, axis=-1, keepdims=True)
    i1 = jnp.min(jnp.where(logits == v1, lane, 128), axis=-1, keepdims=True)
    rest = jnp.where(lane == i1, neg, logits)
    v2 = jnp.max(rest, axis=-1, keepdims=True)
    i2 = jnp.min(jnp.where(rest == v2, lane, 128), axis=-1, keepdims=True)
    e = jnp.exp(v2 - v1)
    w1 = 1.0 / (1.0 + e)
    w2 = e / (1.0 + e)
    out_lane = lax.broadcasted_iota(jnp.int32, (TM, 8), 1)
    r = jnp.where(out_lane == 0, i1.astype(F32), 0.0)
    r = jnp.where(out_lane == 1, i2.astype(F32), r)
    r = jnp.where(out_lane == 2, w1, r)
    r = jnp.where(out_lane == 3, w2, r)
    r_out[...] = r


RUN_ALIGN = 8
RUN_BITS = tuple(1 << b for b in range(TM.bit_length() - 1, RUN_ALIGN.bit_length() - 2, -1))
LOCAL_ROWS = 2 * TM + 64


def _run_copies(run_ref, local, remote, sem, *, to_remote, wait):
    for e in range(N_EXPERTS):
        l0, n, g0 = run_ref[0, 0, e], run_ref[0, 1, e], run_ref[0, 2, e]
        done = jnp.int32(0)
        for bit in RUN_BITS:
            take = (n & bit) != 0

            @pl.when(take)
            def _(done=done, bit=bit):
                loc = local.at[pl.ds(pl.multiple_of(l0 + done, RUN_ALIGN), bit)]
                rem = remote.at[pl.ds(pl.multiple_of(g0 + done, RUN_ALIGN), bit)]
                cp = pltpu.make_async_copy(loc, rem, sem) if to_remote else pltpu.make_async_copy(rem, loc, sem)
                if wait:
                    cp.wait()
                else:
                    cp.start()

            done = done + (n & bit)


def _block_onehot(lp, shape, axis):
    return jnp.where(lax.broadcasted_iota(jnp.int32, shape, axis) == lp, 1.0, 0.0).astype(BF16)


def _dispatch_kernel(run_ref, prev_ref, x_ref, mod_ref, g_ref, lp_ref, init_hbm, xs_hbm, sbuf, sems):
    del init_hbm
    t = pl.program_id(0)
    slot = lax.rem(t, 2)
    h = _modulate(x_ref[...], g_ref[...], mod_ref[3:4, :], mod_ref[4:5, :]).astype(BF16)
    lp = lp_ref[...]
    place = _block_onehot(lp[0:1, :], (LOCAL_ROWS, TM), 0) + _block_onehot(lp[1:2, :], (LOCAL_ROWS, TM), 0)
    sbuf[slot] = jnp.dot(place, h, preferred_element_type=F32)

    @pl.when(t > 0)
    def _():
        _run_copies(prev_ref, sbuf.at[1 - slot], xs_hbm, sems.at[1 - slot], to_remote=True, wait=True)

    _run_copies(run_ref, sbuf.at[slot], xs_hbm, sems.at[slot], to_remote=True, wait=False)

    @pl.when(t == pl.num_programs(0) - 1)
    def _():
        _run_copies(run_ref, sbuf.at[slot], xs_hbm, sems.at[slot], to_remote=True, wait=True)


def _expert_kernel(meta_ref, xs_ref, wgu_ref, wd_ref, y_ref):
    t = pl.program_id(0)
    n_used = meta_ref[pl.num_programs(0)]

    @pl.when(t < n_used)
    def _():
        y_ref[...] = _swiglu_rows(xs_ref[...].astype(BF16), wgu_ref, wd_ref)

    @pl.when(t >= n_used)
    def _():
        y_ref[...] = jnp.zeros_like(y_ref)


def _combine_kernel(run_ref, next_ref, y_hbm, x_ref, mod_ref, r_ref, lp_ref, fin_ref, o_ref, ybuf, sems, *, final):
    t = pl.program_id(0)
    slot = lax.rem(t, 2)

    def fetch(table, s):
        ybuf[s, 2 * TM:LOCAL_ROWS, :] = jnp.zeros((LOCAL_ROWS - 2 * TM, D), F32)
        _run_copies(table, ybuf.at[s], y_hbm, sems.at[s], to_remote=False, wait=False)

    @pl.when(t == 0)
    def _():
        fetch(run_ref, 0)

    @pl.when(t + 1 < pl.num_programs(0))
    def _():
        fetch(next_ref, 1 - slot)

    _run_copies(run_ref, ybuf.at[slot], y_hbm, sems.at[slot], to_remote=False, wait=True)
    ys = ybuf[slot].astype(BF16)
    lp = lp_ref[...]
    y0 = jnp.dot(_block_onehot(lp[:, 0:1], (TM, LOCAL_ROWS), 1), ys, preferred_element_type=F32)
    y1 = jnp.dot(_block_onehot(lp[:, 1:2], (TM, LOCAL_ROWS), 1), ys, preferred_element_type=F32)
    r = r_ref[...]
    x = x_ref[...] + mod_ref[5:6, :] * (r[:, 2:3] * y0 + r[:, 3:4] * y1)
    if final:
        x = (x * lax.rsqrt(jnp.mean(x * x, axis=-1, keepdims=True) + EPS)) * fin_ref[...]
    o_ref[...] = x


def _moe_ffn(x, mods, norm_g, w_router, w_gu, w_down, n_ctx, final_g=None):
    B, T, _ = x.shape
    nt = T // TM
    nct = n_ctx // TM
    n_tok = B * T
    nblk = B * nt
    wr = jnp.zeros((D, 128), F32).at[:, :N_EXPERTS].set(w_router)
    wr_hi = wr.astype(BF16)
    route = pl.pallas_call(
        _router_kernel,
        out_shape=_sds((B, T, 8), F32),
        grid=(B, nt),
        in_specs=[_x_spec(), _mod_spec(nct), _const_spec((1, D)), _const_spec((D, 128)), _const_spec((D, 128))],
        out_specs=pl.BlockSpec((None, TM, 8), lambda b, i: (b, i, 0)),
        compiler_params=_params(("parallel", "parallel"), 32),
        name="router",
    )(x, mods, norm_g.reshape(1, D), wr_hi, (wr - wr_hi.astype(F32)).astype(BF16))

    ids = jnp.arange(N_EXPERTS, dtype=jnp.int32)
    eid = route.reshape(nblk, TM, 8)[:, :, :2].astype(jnp.int32).reshape(nblk, 2 * TM)
    onehot = (eid[:, :, None] == ids).astype(jnp.int32)
    cnt = jnp.sum(onehot, axis=1)
    cnt = ((cnt + RUN_ALIGN - 1) // RUN_ALIGN) * RUN_ALIGN
    l0 = jnp.cumsum(cnt, axis=1) - cnt
    rank = jnp.sum((jnp.cumsum(onehot, axis=1) - onehot) * onehot, axis=2)
    lp = jnp.sum(onehot * l0[:, None, :], axis=2) + rank
    counts = jnp.sum(cnt, axis=0)
    padded = ((counts + TME - 1) // TME) * TME
    ends = jnp.cumsum(padded)
    g0 = (ends - padded)[None, :] + jnp.cumsum(cnt, axis=0) - cnt
    runs = jnp.stack([l0, cnt, g0], axis=1)
    lp_cols = lp.reshape(nblk, TM, 2)
    lp_rows = lp_cols.transpose(0, 2, 1)
    n_tiles = -(-(2 * n_tok + nblk * N_EXPERTS * (RUN_ALIGN - 1) + N_EXPERTS * (TME - 1)) // TME)
    n_rows = n_tiles * TME
    tile_row = jnp.arange(n_tiles, dtype=jnp.int32)[:, None] * TME
    tile_exp = jnp.minimum(jnp.sum((ends[None, :] <= tile_row).astype(jnp.int32), axis=1), N_EXPERTS - 1)
    meta = jnp.concatenate([tile_exp, (ends[-1:] // TME).astype(jnp.int32)])

    def specs(nto, off):
        blk = lambda t: (t // nto) * nt + t % nto + off
        run = lambda shift, last: pl.BlockSpec(
            (1, 3, N_EXPERTS), lambda t: (blk(jnp.clip(t + shift, 0, last)), 0, 0), memory_space=pltpu.SMEM)
        tok = lambda width: pl.BlockSpec((None, TM, width), lambda t: (t // nto, t % nto + off, 0))
        mod = pl.BlockSpec((None, None, 6, D), lambda t: (t // nto, jnp.where(t % nto + off >= nct, 1, 0), 0, 0))
        return blk, run, tok, mod

    blk, run, tok, mod = specs(nt, 0)
    xs = pl.pallas_call(
        _dispatch_kernel,
        out_shape=_sds((n_rows, D), F32),
        grid=(nblk,),
        in_specs=[run(0, nblk - 1), run(-1, nblk - 1), tok(D), mod, pl.BlockSpec((1, D), lambda t: (0, 0)),
                  pl.BlockSpec((None, 2, TM), lambda t: (blk(t), 0, 0)),
                  pl.BlockSpec(memory_space=pl.ANY)],
        out_specs=pl.BlockSpec(memory_space=pl.ANY),
        scratch_shapes=[pltpu.VMEM((2, LOCAL_ROWS, D), F32), pltpu.SemaphoreType.DMA((2,))],
        input_output_aliases={6: 0},
        compiler_params=_params(("arbitrary",), 40),
        name="moe_dispatch",
    )(runs, runs, x, mods, norm_g.reshape(1, D), lp_rows, jnp.zeros((n_rows, D), F32))

    y = pl.pallas_call(
        _expert_kernel,
        out_shape=_sds((n_rows, D), F32),
        grid_spec=pltpu.PrefetchScalarGridSpec(
            num_scalar_prefetch=1,
            grid=(n_tiles,),
            in_specs=[pl.BlockSpec((TME, D), lambda t, meta: (t, 0)),
                      pl.BlockSpec((None, D, 2 * D_FF), lambda t, meta: (meta[t], 0, 0)),
                      pl.BlockSpec((None, D_FF, D), lambda t, meta: (meta[t], 0, 0))],
            out_specs=pl.BlockSpec((TME, D), lambda t, meta: (t, 0))),
        compiler_params=_params(("arbitrary",), 56),
        name="experts",
    )(meta, xs, w_gu.astype(BF16), w_down.astype(BF16))

    final = final_g is not None
    off = nct if final else 0
    nto = nt - off
    n_steps = B * nto
    blk, run, tok, mod = specs(nto, off)
    return pl.pallas_call(
        functools.partial(_combine_kernel, final=final),
        out_shape=_sds((B, nto * TM, D), F32),
        grid=(n_steps,),
        in_specs=[run(0, n_steps - 1), run(1, n_steps - 1), pl.BlockSpec(memory_space=pl.ANY), tok(D), mod, tok(8),
                  pl.BlockSpec((None, TM, 2), lambda t: (blk(t), 0, 0)),
                  pl.BlockSpec((1, D), lambda t: (0, 0))],
        out_specs=pl.BlockSpec((None, TM, D), lambda t: (t // nto, t % nto, 0)),
        scratch_shapes=[pltpu.VMEM((2, LOCAL_ROWS, D), F32), pltpu.SemaphoreType.DMA((2,))],
        compiler_params=_params(("arbitrary",), 40),
        name="moe_combine",
    )(runs, runs, y, x, mods, route, lp_cols, (final_g if final else norm_g).reshape(1, D))


def kernel(x, c, ctx, c_ctx, l0_ada_w, l0_ada_b, l0_norm_mix, l0_norm_ffn, l0_conv_w_in, l0_conv_w, l0_conv_w_out, l0_ffn_w_gu, l0_ffn_w_down, l1_ada_w, l1_ada_b, l1_norm_mix, l1_norm_ffn, l1_attn_w_qkv, l1_attn_lam, l1_attn_subln, l1_attn_w_o, l1_moe_router, l1_moe_w_gu, l1_moe_w_down, l2_ada_w, l2_ada_b, l2_norm_mix, l2_norm_ffn, l2_ssm_a_re, l2_ssm_a_im, l2_ssm_log_dt, l2_ssm_b_re, l2_ssm_b_im, l2_ssm_c_re, l2_ssm_c_im, l2_ssm_d, l2_ssm_w_glu, l2_ffn_w_gu, l2_ffn_w_down, l3_ada_w, l3_ada_b, l3_norm_mix, l3_norm_ffn, l3_conv_w_in, l3_conv_w, l3_conv_w_out, l3_moe_router, l3_moe_w_gu, l3_moe_w_down, final_norm):
    B, n_lat, _ = x.shape
    n_ctx = ctx.shape[1]
    T = n_ctx + n_lat
    assert n_ctx % TM == 0 and n_lat % TM == 0 and n_lat % GRID_W == 0 and B < 8
    xs = jnp.concatenate([ctx, x], axis=1)
    cond8 = jnp.concatenate([c, c_ctx[None], jnp.zeros((8 - B - 1, D), F32)], axis=0)

    def mods_of(w, b):
        m = _ada(cond8, w, b).reshape(8, 6, D)
        return jnp.stack([jnp.broadcast_to(m[B], (B, 6, D)), m[:B]], axis=1)

    mods = mods_of(l0_ada_w, l0_ada_b)
    xs = _conv_mixer(xs, mods, l0_norm_mix, l0_conv_w_in, l0_conv_w, l0_conv_w_out, n_ctx)
    xs = _dense_ffn(xs, mods, l0_norm_ffn, l0_ffn_w_gu, l0_ffn_w_down, n_ctx)

    mods = mods_of(l1_ada_w, l1_ada_b)
    xs = _attn_mixer(xs, mods, l1_norm_mix, l1_attn_w_qkv, l1_attn_lam, l1_attn_subln, l1_attn_w_o, n_ctx,
                     0.8 - 0.6 * math.exp(-0.3 * 1))
    xs = _moe_ffn(xs, mods, l1_norm_ffn, l1_moe_router, l1_moe_w_gu, l1_moe_w_down, n_ctx)

    mods = mods_of(l2_ada_w, l2_ada_b)
    xs = _s5_mixer(xs, mods, l2_norm_mix,
                   (l2_ssm_a_re, l2_ssm_a_im, l2_ssm_log_dt, l2_ssm_b_re, l2_ssm_b_im, l2_ssm_c_re, l2_ssm_c_im,
                    l2_ssm_d, l2_ssm_w_glu), n_ctx)
    xs = _dense_ffn(xs, mods, l2_norm_ffn, l2_ffn_w_gu, l2_ffn_w_down, n_ctx)

    mods = mods_of(l3_ada_w, l3_ada_b)
    xs = _conv_mixer(xs, mods, l3_norm_mix, l3_conv_w_in, l3_conv_w, l3_conv_w_out, n_ctx)
    return _moe_ffn(xs, mods, l3_norm_ffn, l3_moe_router, l3_moe_w_gu, l3_moe_w_down, n_ctx, final_g=final_norm)
```

```python
import functools
import math

import jax
import jax.numpy as jnp
from jax import lax
from jax.experimental import pallas as pl
from jax.experimental.pallas import tpu as pltpu

F32 = jnp.float32
BF16 = jnp.bfloat16
HIGHEST = lax.Precision.HIGHEST

D = 1024
GRID_W = 64
N_HEADS = 8
HEAD_DIM = 64
V_DIM = 128
ROPE_BASE = 10000.0
GROUP_SIZE = 16
N_GROUPS = 64
STATE_DIM = 64
D_FF = 2816
N_EXPERTS = 8
EPS = 1e-6

TM = 256
FF_CHUNK = 256
TK = 256
S5_T = 16
TME = 512
MIB = 1 << 20


def _sds(shape, dtype):
    return jax.ShapeDtypeStruct(shape, dtype)


def _params(sem, vmem_mib):
    return pltpu.CompilerParams(dimension_semantics=sem, vmem_limit_bytes=vmem_mib * MIB)


def _modulate(x, g, shift, scale):
    y = x * lax.rsqrt(jnp.mean(x * x, axis=-1, keepdims=True) + EPS)
    return (y * g) * (1.0 + scale) + shift


def _x_spec():
    return pl.BlockSpec((None, TM, D), lambda b, i: (b, i, 0))


def _mod_spec(n_ctx_tiles):
    return pl.BlockSpec((None, None, 6, D), lambda b, i: (b, jnp.where(i >= n_ctx_tiles, 1, 0), 0, 0))


def _const_spec(shape):
    nd = len(shape)
    return pl.BlockSpec(shape, lambda b, i: (0,) * nd)


def _pair_spec():
    return pl.BlockSpec((2, TM, D), lambda b, i: (b, i, 0))


def _pair_mod_spec(n_ctx_tiles):
    return pl.BlockSpec((2, None, 6, D), lambda b, i: (b, jnp.where(i >= n_ctx_tiles, 1, 0), 0, 0))


def _modulate_pair(x_ref, g_ref, mod_ref, shift_row, scale_row):
    return jnp.concatenate(
        [_modulate(x_ref[j], g_ref[...], mod_ref[j, shift_row:shift_row + 1, :], mod_ref[j, scale_row:scale_row + 1, :])
         for j in range(2)], axis=0).astype(BF16)


def _ada_kernel(c_ref, w_ref, b_ref, o_ref):
    c = c_ref[...]
    s = c * jax.nn.sigmoid(c)
    o_ref[...] = jnp.dot(s, w_ref[...], preferred_element_type=F32, precision=HIGHEST) + b_ref[...]


def _ada(cond8, w, b):
    nb = 4
    bn = w.shape[1] // nb
    return pl.pallas_call(
        _ada_kernel,
        out_shape=_sds((8, w.shape[1]), F32),
        grid=(nb,),
        in_specs=[pl.BlockSpec((8, D), lambda j: (0, 0)),
                  pl.BlockSpec((D, bn), lambda j: (0, j)),
                  pl.BlockSpec((1, bn), lambda j: (0, j))],
        out_specs=pl.BlockSpec((8, bn), lambda j: (0, j)),
        compiler_params=_params(("parallel",), 40),
        name="ada",
    )(cond8, w, b.reshape(1, -1))


def _conv_in_kernel(x_ref, mod_ref, g_ref, w_ref, b_out, z_out):
    h = _modulate_pair(x_ref, g_ref, mod_ref, 0, 1)
    b_gate = jnp.dot(h, w_ref[:, 0:D], preferred_element_type=F32).astype(BF16)
    c_gate = jnp.dot(h, w_ref[:, D:2 * D], preferred_element_type=F32)
    v = jnp.dot(h, w_ref[:, 2 * D:3 * D], preferred_element_type=F32)
    z = c_gate * v
    for j in range(2):
        b_out[j] = b_gate[j * TM:(j + 1) * TM, :]
        z_out[j] = z[j * TM:(j + 1) * TM, :]


def _conv_out_kernel(z_ref, zp_ref, zn_ref, b_ref, cw_ref, w_ref, x_ref, mod_ref, o_ref, *, n_ctx, n_tot):
    i = pl.program_id(1)
    z = z_ref[...]
    row = lax.broadcasted_iota(jnp.int32, (TM, 1), 0)
    grow = row + i * TM
    prev_row = zp_ref[7:8, :]
    next_row = zn_ref[0:1, :]
    zm1 = jnp.where(row == 0, prev_row, pltpu.roll(z, 1, axis=0))
    zp1 = jnp.where(row == TM - 1, next_row, pltpu.roll(z, TM - 1, axis=0))
    zm1 = jnp.where((grow == 0) | (grow == n_ctx), 0.0, zm1)
    zp1 = jnp.where((grow == n_ctx - 1) | (grow == n_tot - 1), 0.0, zp1)
    y = cw_ref[0:1, :] * zm1 + cw_ref[1:2, :] * z + cw_ref[2:3, :] * zp1
    a = (b_ref[...].astype(F32) * y).astype(BF16)
    o_ref[...] = x_ref[...] + mod_ref[2:3, :] * jnp.dot(a, w_ref[...], preferred_element_type=F32)


def _conv_mixer(x, mods, norm_g, w_in, conv_w, w_out, n_ctx):
    B, T, _ = x.shape
    nt = T // TM
    nct = n_ctx // TM
    b_gate, z = pl.pallas_call(
        _conv_in_kernel,
        out_shape=(_sds((B, T, D), BF16), _sds((B, T, D), F32)),
        grid=(B // 2, nt),
        in_specs=[_pair_spec(), _pair_mod_spec(nct), _const_spec((1, D)), _const_spec((D, 3 * D))],
        out_specs=(_pair_spec(), _pair_spec()),
        compiler_params=_params(("parallel", "parallel"), 48),
        name="conv_in",
    )(x, mods, norm_g.reshape(1, D), w_in.astype(BF16))
    r8 = TM // 8
    return pl.pallas_call(
        functools.partial(_conv_out_kernel, n_ctx=n_ctx, n_tot=T),
        out_shape=_sds((B, T, D), F32),
        grid=(B, nt),
        in_specs=[_x_spec(),
                  pl.BlockSpec((None, 8, D), lambda b, i: (b, jnp.maximum(i * r8 - 1, 0), 0)),
                  pl.BlockSpec((None, 8, D), lambda b, i: (b, jnp.minimum((i + 1) * r8, T // 8 - 1), 0)),
                  _x_spec(), _const_spec((3, D)), _const_spec((D, D)), _x_spec(), _mod_spec(nct)],
        out_specs=_x_spec(),
        compiler_params=_params(("parallel", "parallel"), 48),
        name="conv_out",
    )(z, z, z, b_gate, conv_w, w_out.astype(BF16), x, mods)


def _qkv_kernel(x_ref, mod_ref, g_ref, w_ref, cos_ref, slo_ref, shi_ref, qt_out, k_out, vt_out):
    h = _modulate_pair(x_ref, g_ref, mod_ref, 0, 1)
    twice = lambda tab: jnp.concatenate([tab[...], tab[...]], axis=0)
    cos, s_lo, s_hi = twice(cos_ref), twice(slo_ref), twice(shi_ref)

    def rope(t):
        return t * cos + pltpu.roll(t, 128 - 16, axis=1) * s_lo + pltpu.roll(t, 16, axis=1) * s_hi

    q_scale = (HEAD_DIM ** -0.5) * math.log2(math.e)
    for hd in range(N_HEADS):
        lo, hi = hd * 128, (hd + 1) * 128
        q = rope(jnp.dot(h, w_ref[:, lo:hi], preferred_element_type=F32) * q_scale)
        k = rope(jnp.dot(h, w_ref[:, D + lo:D + hi], preferred_element_type=F32)).astype(BF16)
        v = jnp.dot(h, w_ref[:, 2 * D + lo:2 * D + hi], preferred_element_type=F32).astype(BF16).astype(F32)
        for j in range(2):
            rows = slice(j * TM, (j + 1) * TM)
            qt_out[j, hd] = q[rows, :].T.astype(BF16)
            k_out[j, :, lo:hi] = k[rows, :]
            vt_out[j, hd] = v[rows, :].T.astype(BF16)


def _attn_kernel(qt_ref, k_ref, vt_ref, lam_ref, sub_ref, o_ref, s_buf, acc_ref, *,
                 n_ctx_tiles, n_lat_chunks, lam_init):
    i = pl.program_id(2)
    qt = qt_ref[...]
    row = lax.broadcasted_iota(jnp.int32, qt.shape, 0)
    zero = jnp.zeros_like(qt)
    q_cat = jnp.concatenate([jnp.where(row < HEAD_DIM, qt, zero), jnp.where(row >= HEAD_DIM, qt, zero)], axis=1)

    def scores(first_tile, n_tiles):
        start = first_tile * TM
        if not isinstance(first_tile, int):
            start = pl.multiple_of(start, TM)
        s = jnp.dot(k_ref[pl.ds(start, n_tiles * TM), :], q_cat, preferred_element_type=F32)
        return s, jnp.max(s, axis=0, keepdims=True)

    def absorb(s, s_max, first_tile, n_tiles, m, l):
        m_new = jnp.maximum(m, s_max)
        p = jnp.exp2(s - m_new)
        alpha = jnp.exp2(m - m_new)
        l_new = alpha * l + jnp.sum(p, axis=0, keepdims=True)
        pb = p.astype(BF16)
        pv = jnp.dot(vt_ref[first_tile], pb[0:TM, :], preferred_element_type=F32)
        for c in range(1, n_tiles):
            pv = pv + jnp.dot(vt_ref[first_tile + c], pb[c * TM:(c + 1) * TM, :], preferred_element_type=F32)
        acc_ref[...] = alpha * acc_ref[...] + pv
        return m_new, l_new

    def finish(l):
        acc = acc_ref[...]
        lv = lam_ref[...]
        lam = (jnp.exp(jnp.sum(lv[0:1, :] * lv[1:2, :], axis=-1, keepdims=True))
               - jnp.exp(jnp.sum(lv[2:3, :] * lv[3:4, :], axis=-1, keepdims=True)) + lam_init)
        o = (acc[:, 0:TM] / l[:, 0:TM] - lam * (acc[:, TM:2 * TM] / l[:, TM:2 * TM])).T
        o = o * lax.rsqrt(jnp.mean(o * o, axis=-1, keepdims=True) + EPS)
        o_ref[...] = ((o * sub_ref[...]) * (1.0 - lam_init)).astype(BF16)

    acc_ref[...] = jnp.zeros_like(acc_ref)
    s_ctx, max_ctx = scores(0, n_ctx_tiles)
    s_buf[0], max_first = scores(n_ctx_tiles, 2)
    m_ctx, l_ctx = absorb(s_ctx, max_ctx, 0, n_ctx_tiles,
                          jnp.full((1, 2 * TM), -1e30, F32), jnp.zeros((1, 2 * TM), F32))

    @pl.when(i < n_ctx_tiles)
    def _():
        finish(l_ctx)

    @pl.when(i >= n_ctx_tiles)
    def _():
        nbuf = s_buf.shape[0]
        m, l, maxes = m_ctx, l_ctx, [max_first] + [None] * (nbuf - 1)
        for c in range(n_lat_chunks):
            if c + 1 < n_lat_chunks:
                s_buf[(c + 1) % nbuf], maxes[(c + 1) % nbuf] = scores(n_ctx_tiles + 2 * (c + 1), 2)
            m, l = absorb(s_buf[c % nbuf], maxes[c % nbuf], n_ctx_tiles + 2 * c, 2, m, l)
        finish(l)


def _out_proj_kernel(a_ref, w_ref, x_ref, mod_ref, o_ref):
    a = jnp.concatenate([a_ref[0], a_ref[1]], axis=0)
    y = jnp.dot(a, w_ref[...], preferred_element_type=F32)
    for j in range(2):
        o_ref[j] = x_ref[j] + mod_ref[j, 2:3, :] * y[j * TM:(j + 1) * TM, :]


def _rope_tables(n_ctx, n_lat):
    t = jnp.arange(n_lat, dtype=jnp.int32)
    row = (t // GRID_W).astype(F32)
    col = (t % GRID_W).astype(F32)
    n_freq = HEAD_DIM // 4
    inv = ROPE_BASE ** (-jnp.arange(n_freq, dtype=F32) / n_freq)
    ang_r = row[:, None] * inv
    ang_c = col[:, None] * inv
    zero = jnp.zeros_like(ang_r)
    cos64 = jnp.concatenate([jnp.cos(ang_r), jnp.cos(ang_r), jnp.cos(ang_c), jnp.cos(ang_c)], axis=-1)
    slo64 = jnp.concatenate([-jnp.sin(ang_r), zero, -jnp.sin(ang_c), zero], axis=-1)
    shi64 = jnp.concatenate([zero, jnp.sin(ang_r), zero, jnp.sin(ang_c)], axis=-1)

    def full(tab, ctx_val):
        tab = jnp.concatenate([tab, tab], axis=-1)
        return jnp.concatenate([jnp.full((n_ctx, 128), ctx_val, F32), tab], axis=0)

    return full(cos64, 1.0), full(slo64, 0.0), full(shi64, 0.0)


def _attn_mixer(x, mods, norm_g, w_qkv, lam_vecs, subln, w_o, n_ctx, lam_init):
    B, T, _ = x.shape
    nt = T // TM
    nct = n_ctx // TM
    cos, s_lo, s_hi = _rope_tables(n_ctx, T - n_ctx)
    tab_spec = pl.BlockSpec((TM, 128), lambda b, i: (i, 0))
    assert (T - n_ctx) % (4 * TM) == 0
    t_spec = pl.BlockSpec((2, N_HEADS, None, 128, TM), lambda b, i: (b, 0, i, 0, 0))
    t_shape = _sds((B, N_HEADS, nt, 128, TM), BF16)
    qt, k, vt = pl.pallas_call(
        _qkv_kernel,
        out_shape=(t_shape, _sds((B, T, D), BF16), t_shape),
        grid=(B // 2, nt),
        in_specs=[_pair_spec(), _pair_mod_spec(nct), _const_spec((1, D)), _const_spec((D, 3 * D)),
                  tab_spec, tab_spec, tab_spec],
        out_specs=(t_spec, _pair_spec(), t_spec),
        compiler_params=_params(("parallel", "parallel"), 48),
        name="qkv",
    )(x, mods, norm_g.reshape(1, D), w_qkv.astype(BF16), cos, s_lo, s_hi)

    attn = pl.pallas_call(
        functools.partial(_attn_kernel, n_ctx_tiles=nct, n_lat_chunks=(T - n_ctx) // (2 * TM), lam_init=lam_init),
        out_shape=_sds((B, T, D), BF16),
        grid=(B, N_HEADS, nt),
        in_specs=[pl.BlockSpec((None, None, None, 128, TM), lambda b, h, i: (b, h, i, 0, 0)),
                  pl.BlockSpec((None, T, 128), lambda b, h, i: (b, 0, h)),
                  pl.BlockSpec((None, None, nt, 128, TM), lambda b, h, i: (b, h, 0, 0, 0)),
                  pl.BlockSpec((4, HEAD_DIM), lambda b, h, i: (0, 0)),
                  pl.BlockSpec((1, V_DIM), lambda b, h, i: (0, 0))],
        out_specs=pl.BlockSpec((None, TM, 128), lambda b, h, i: (b, i, h)),
        scratch_shapes=[pltpu.VMEM((2, 2 * TM, 2 * TM), F32), pltpu.VMEM((V_DIM, 2 * TM), F32)],
        compiler_params=_params(("parallel", "parallel", "parallel"), 48),
        name="diff_attn",
    )(qt, k, vt, lam_vecs, subln.reshape(1, V_DIM))

    return pl.pallas_call(
        _out_proj_kernel,
        out_shape=_sds((B, T, D), F32),
        grid=(B // 2, nt),
        in_specs=[_pair_spec(), _const_spec((D, D)), _pair_spec(), _pair_mod_spec(nct)],
        out_specs=_pair_spec(),
        compiler_params=_params(("parallel", "parallel"), 48),
        name="attn_out",
    )(attn, w_o.astype(BF16), x, mods)


def _mod_only_kernel(x_ref, mod_ref, g_ref, h_out):
    h_out[...] = _modulate(x_ref[...], g_ref[...], mod_ref[0:1, :], mod_ref[1:2, :])


def _s5_chunk_rows(h_ref):
    return jnp.concatenate([h_ref[:, s, :] for s in range(S5_T)], axis=-1).astype(BF16)


def _s5_state_kernel(h_ref, w_ref, sfr, sfi, sbr, sbi):
    s = jnp.dot(_s5_chunk_rows(h_ref), w_ref[...], preferred_element_type=F32)
    sfr[...] = s[:, 0:512]
    sfi[...] = s[:, 512:1024]
    sbr[...] = s[:, 1024:1536]
    sbi[...] = s[:, 1536:2048]


def _s5_scan_kernel(sfr, sfi, sbr, sbi, afr, afi, abr, abi, hfr, hfi, hbr, hbi, *, n_ctx_chunks, n_chunks):
    ar, ai = afr[...], afi[...]

    def fwd(n, carry):
        hr, hi = carry
        hfr[n] = hr
        hfi[n] = hi
        sr, si = sfr[n], sfi[n]
        return ar * hr - ai * hi + sr, ar * hi + ai * hr + si

    zero = jnp.zeros(ar.shape, F32)
    lax.fori_loop(0, n_chunks, fwd, (zero, zero))

    br, bi = abr[...], abi[...]

    def bwd(t, carry, top):
        n = top - t
        hr, hi = carry
        hbr[n] = hr
        hbi[n] = hi
        sr, si = sbr[n], sbi[n]
        return br * hr - bi * hi + sr, br * hi + bi * hr + si

    carry = lax.fori_loop(0, n_ctx_chunks, functools.partial(bwd, top=n_ctx_chunks - 1), (zero, zero))
    lax.fori_loop(0, n_chunks - n_ctx_chunks, functools.partial(bwd, top=n_chunks - 1), carry)


def _s5_out_kernel(h_ref, m_ref, hfr, hfi, hbr, hbi, v_ref, y_ref):
    state = jnp.concatenate([hfr[...], hfi[...], hbr[...], hbi[...]], axis=-1).astype(BF16)
    y = (jnp.dot(_s5_chunk_rows(h_ref), m_ref[...], preferred_element_type=F32)
         + jnp.dot(state, v_ref[...], preferred_element_type=F32))
    for t in range(S5_T // 2):
        y_ref[:, t, :] = y[:, t * 128:(t + 1) * 128]


def _glu_kernel(y_ref, x_ref, mod_ref, g_ref, d_ref, w_ref, o_ref):
    def act(j):
        hl = _modulate(x_ref[j], g_ref[...], mod_ref[j, 0:1, :], mod_ref[j, 1:2, :])
        return jax.nn.gelu(y_ref[j] + d_ref[...] * hl).astype(BF16)

    a = jnp.concatenate([act(0), act(1)], axis=0)
    val = jnp.dot(a, w_ref[:, 0:D], preferred_element_type=F32)
    gate = jnp.dot(a, w_ref[:, D:2 * D], preferred_element_type=F32)
    out = val * jax.nn.sigmoid(gate)
    for j in range(2):
        o_ref[j] = x_ref[j] + mod_ref[j, 2:3, :] * out[j * TM:(j + 1) * TM, :]


def _s5_matrices(a_re, a_im, log_dt, b_re, b_im, c_re, c_im):
    T = S5_T
    hp = functools.partial(jnp.einsum, precision=HIGHEST)
    pw_re, pw_im, bb_re, bb_im = [], [], [], []
    for d in range(2):
        ar, ai = a_re[d].astype(F32), a_im[d].astype(F32)
        dt = jnp.exp(log_dt[d].astype(F32))[:, None]
        mag = jnp.exp(dt * ar)
        ab_re = mag * jnp.cos(dt * ai)
        ab_im = mag * jnp.sin(dt * ai)
        den = ar * ar + ai * ai
        nr = ab_re - 1.0
        co_re = (nr * ar + ab_im * ai) / den
        co_im = (ab_im * ar - nr * ai) / den
        br, bi = b_re[d].astype(F32), b_im[d].astype(F32)
        bb_re.append(co_re[..., None] * br - co_im[..., None] * bi)
        bb_im.append(co_re[..., None] * bi + co_im[..., None] * br)
        pr = [jnp.ones_like(ab_re)]
        pi = [jnp.zeros_like(ab_re)]
        for _ in range(T):
            pr.append(pr[-1] * ab_re - pi[-1] * ab_im)
            pi.append(pr[-2] * ab_im + pi[-1] * ab_re)
        pw_re.append(jnp.stack(pr))
        pw_im.append(jnp.stack(pi))

    def lag_kernels(d):
        mr = pw_re[d][:T, :, :, None] * bb_re[d][None] - pw_im[d][:T, :, :, None] * bb_im[d][None]
        mi = pw_re[d][:T, :, :, None] * bb_im[d][None] + pw_im[d][:T, :, :, None] * bb_re[d][None]
        return hp('gop,kgpi->kgoi', c_re[d].astype(F32), mr) - hp('gop,kgpi->kgoi', c_im[d].astype(F32), mi)

    kf, kb = lag_kernels(0), lag_kernels(1)
    lag0 = (kf[0] + kb[0])[None]
    kfull = jnp.concatenate([kb[:0:-1], lag0, kf[1:]], axis=0)
    tt = jnp.arange(T)
    idx = tt[None, :] - tt[:, None] + (T - 1)
    mt = kfull[idx]
    mt = mt.transpose(2, 0, 4, 1, 3).reshape(N_GROUPS, T * GROUP_SIZE, T * GROUP_SIZE)

    def end_state(d, powers):
        pr, pi = pw_re[d][powers], pw_im[d][powers]
        wr = pr[..., None] * bb_re[d][None] - pi[..., None] * bb_im[d][None]
        wi = pr[..., None] * bb_im[d][None] + pi[..., None] * bb_re[d][None]
        f = lambda w: w.transpose(1, 0, 3, 2).reshape(N_GROUPS, T * GROUP_SIZE, STATE_DIM)
        return f(wr), f(wi)

    def read_out(d, powers):
        pr, pi = pw_re[d][powers], pw_im[d][powers]
        cr, ci = c_re[d].astype(F32), c_im[d].astype(F32)
        qr = cr[None] * pr[:, :, None, :] - ci[None] * pi[:, :, None, :]
        qi = cr[None] * pi[:, :, None, :] + ci[None] * pr[:, :, None, :]
        f = lambda q: q.transpose(1, 3, 0, 2).reshape(N_GROUPS, STATE_DIM, T * GROUP_SIZE)
        return f(qr), f(-qi)

    wfr, wfi = end_state(0, T - 1 - tt)
    wbr, wbi = end_state(1, tt)
    vfr, vfi = read_out(0, tt + 1)
    vbr, vbi = read_out(1, T - tt)

    gsz = T * GROUP_SIZE
    g8 = jnp.arange(8, dtype=jnp.int32)[:, None]
    k = jnp.arange(gsz, dtype=jnp.int32)[None, :]
    chunk_col = (k // GROUP_SIZE) * 128 + g8 * GROUP_SIZE + k % GROUP_SIZE
    state_col = (k // STATE_DIM) * (8 * STATE_DIM) + g8 * STATE_DIM + k % STATE_DIM
    lanes = jnp.arange(8 * gsz, dtype=jnp.int32)[None, None, :]
    to_chunk = (chunk_col[:, :, None] == lanes).astype(BF16)
    to_state = (state_col[:, :, None] == lanes).astype(BF16)
    w_cat = jnp.stack([wfr, wfi, wbr, wbi], axis=2)
    v_cat = jnp.stack([vfr, vfi, vbr, vbi], axis=1)
    m8 = _spread(mt, to_chunk, T, GROUP_SIZE)
    w8 = _spread(w_cat.reshape(N_GROUPS, gsz, gsz), to_state, T, GROUP_SIZE)
    v8 = _spread(v_cat.reshape(N_GROUPS, gsz, gsz), to_chunk, 4, STATE_DIM)
    a_pow = [t[T].reshape(N_GROUPS // 2, 2 * STATE_DIM) for t in (pw_re[0], pw_im[0], pw_re[1], pw_im[1])]
    return m8, w8, v8, a_pow


def _spread_kernel(a_ref, c_ref, o_ref):
    e = jnp.dot(a_ref[...], c_ref[...], preferred_element_type=F32)
    o_ref[...] = e.reshape(o_ref.shape).astype(BF16)


def _spread(a, col_of, r1, r2):
    nb = N_GROUPS // 8
    kdim, n = col_of.shape[1], col_of.shape[2]
    out = pl.pallas_call(
        _spread_kernel,
        out_shape=_sds((nb, r1, 8, r2, n), BF16),
        grid=(nb, 8),
        in_specs=[pl.BlockSpec((None, None, r1 * r2, kdim), lambda g, j: (g, j, 0, 0)),
                  pl.BlockSpec((None, kdim, n), lambda g, j: (j, 0, 0))],
        out_specs=pl.BlockSpec((None, r1, None, r2, n), lambda g, j: (g, 0, j, 0, 0)),
        compiler_params=_params(("parallel", "parallel"), 32),
        name="s5_spread",
    )(a.astype(BF16).reshape(nb, 8, r1 * r2, kdim), col_of)
    return out.reshape(nb, r1 * 8 * r2, n)


def _s5_mixer(x, mods, norm_g, ssm, n_ctx):
    a_re, a_im, log_dt, b_re, b_im, c_re, c_im, d_skip, w_glu = ssm
    B, T, _ = x.shape
    nt = T // TM
    nct = n_ctx // TM
    nch = T // S5_T
    nb = N_GROUPS // 8
    sw = 8 * STATE_DIM
    cw = S5_T * 128
    m8, w8, v8, a_pow = _s5_matrices(a_re, a_im, log_dt, b_re, b_im, c_re, c_im)

    h = pl.pallas_call(
        _mod_only_kernel,
        out_shape=_sds((B, T, D), F32),
        grid=(B, nt),
        in_specs=[_x_spec(), _mod_spec(nct), _const_spec((1, D))],
        out_specs=_x_spec(),
        compiler_params=_params(("parallel", "parallel"), 32),
        name="s5_mod",
    )(x, mods, norm_g.reshape(1, D))
    h4 = h.reshape(B * nch, S5_T, D)

    rows = B * nch
    state_shape = _sds((rows, nb * sw), F32)
    s_parts = pl.pallas_call(
        _s5_state_kernel,
        out_shape=(state_shape,) * 4,
        grid=(nb, B),
        in_specs=[pl.BlockSpec((nch, S5_T, 128), lambda g, b: (b, 0, g)),
                  pl.BlockSpec((None, cw, 4 * sw), lambda g, b: (g, 0, 0))],
        out_specs=(pl.BlockSpec((nch, sw), lambda g, b: (b, g)),) * 4,
        compiler_params=_params(("parallel", "parallel"), 56),
        name="s5_state",
    )(h4, w8)

    n_pairs = N_GROUPS // 2
    npb = 8
    view = lambda t: t.reshape(B, nch, n_pairs, 128)
    scan_blk = pl.BlockSpec((None, nch, npb, 128), lambda b, j: (b, 0, j, 0))
    coef_blk = pl.BlockSpec((npb, 128), lambda b, j: (j, 0))
    h_parts = pl.pallas_call(
        functools.partial(_s5_scan_kernel, n_ctx_chunks=n_ctx // S5_T, n_chunks=nch),
        out_shape=(_sds((B, nch, n_pairs, 128), F32),) * 4,
        grid=(B, n_pairs // npb),
        in_specs=[scan_blk] * 4 + [coef_blk] * 4,
        out_specs=(scan_blk,) * 4,
        compiler_params=_params(("parallel", "parallel"), 56),
        name="s5_scan",
    )(*[view(t) for t in s_parts], *a_pow)

    half = S5_T // 2
    y4 = pl.pallas_call(
        _s5_out_kernel,
        out_shape=_sds((rows, S5_T, D), F32),
        grid=(nb, 2, B),
        in_specs=[pl.BlockSpec((nch, S5_T, 128), lambda g, hf, b: (b, 0, g)),
                  pl.BlockSpec((None, cw, half * 128), lambda g, hf, b: (g, 0, hf))]
                 + [pl.BlockSpec((nch, sw), lambda g, hf, b: (b, g))] * 4
                 + [pl.BlockSpec((None, 4 * sw, half * 128), lambda g, hf, b: (g, 0, hf))],
        out_specs=pl.BlockSpec((nch, half, 128), lambda g, hf, b: (b, hf, g)),
        compiler_params=_params(("parallel", "parallel", "parallel"), 56),
        name="s5_out",
    )(h4, m8, *[t.reshape(rows, nb * sw) for t in h_parts], v8)
    y = y4.reshape(B, T, D)

    return pl.pallas_call(
        _glu_kernel,
        out_shape=_sds((B, T, D), F32),
        grid=(B // 2, nt),
        in_specs=[_pair_spec(), _pair_spec(), _pair_mod_spec(nct), _const_spec((1, D)), _const_spec((1, D)),
                  _const_spec((D, 2 * D))],
        out_specs=_pair_spec(),
        compiler_params=_params(("parallel", "parallel"), 48),
        name="s5_glu",
    )(y, x, mods, norm_g.reshape(1, D), d_skip.reshape(1, D), w_glu.astype(BF16))


N_FF_CHUNKS = D_FF // FF_CHUNK


def _swiglu_rows(h, wgu_ref, wd_ref, after_chunk=None):
    acc = jnp.zeros((h.shape[0], D), F32)
    for c in range(N_FF_CHUNKS):
        lo, hi = c * FF_CHUNK, (c + 1) * FF_CHUNK
        g = jnp.dot(h, wgu_ref[:, lo:hi], preferred_element_type=F32)
        u = jnp.dot(h, wgu_ref[:, D_FF + lo:D_FF + hi], preferred_element_type=F32)
        a = ((g * jax.nn.sigmoid(g)) * u).astype(BF16)
        acc = acc + jnp.dot(a, wd_ref[lo:hi, :], preferred_element_type=F32)
        if after_chunk is not None:
            after_chunk(c)
    return acc


def _ffn_kernel(x_ref, mod_ref, g_ref, wgu_ref, wd_ref, o_ref):
    y = _swiglu_rows(_modulate_pair(x_ref, g_ref, mod_ref, 3, 4), wgu_ref, wd_ref)
    for j in range(2):
        o_ref[j] = x_ref[j] + mod_ref[j, 5:6, :] * y[j * TM:(j + 1) * TM, :]


def _dense_ffn(x, mods, norm_g, w_gu, w_down, n_ctx):
    B, T, _ = x.shape
    assert B % 2 == 0
    return pl.pallas_call(
        _ffn_kernel,
        out_shape=_sds((B, T, D), F32),
        grid=(B // 2, T // TM),
        in_specs=[_pair_spec(), _pair_mod_spec(n_ctx // TM),
                  _const_spec((1, D)), _const_spec((D, 2 * D_FF)), _const_spec((D_FF, D))],
        out_specs=_pair_spec(),
        compiler_params=_params(("parallel", "parallel"), 56),
        name="dense_ffn",
    )(x, mods, norm_g.reshape(1, D), w_gu.astype(BF16), w_down.astype(BF16))


def _router_kernel(x_ref, mod_ref, g_ref, whi_ref, wlo_ref, r_out):
    h = _modulate(x_ref[...], g_ref[...], mod_ref[3:4, :], mod_ref[4:5, :])
    h_hi = h.astype(BF16)
    h_lo = (h - h_hi.astype(F32)).astype(BF16)
    logits = (jnp.dot(h_hi, whi_ref[...], preferred_element_type=F32)
              + jnp.dot(h_lo, whi_ref[...], preferred_element_type=F32)
              + jnp.dot(h_hi, wlo_ref[...], preferred_element_type=F32))
    lane = lax.broadcasted_iota(jnp.int32, logits.shape, 1)
    neg = jnp.float32(-jnp.inf)
    logits = jnp.where(lane < N_EXPERTS, logits, neg)
    v1 = jnp.max(logits, axis=-1, keepdims=True)
    i1 = jnp.min(jnp.where(logits == v1, lane, 128), axis=-1, keepdims=True)
    rest = jnp.where(lane == i1, neg, logits)
    v2 = jnp.max(rest, axis=-1, keepdims=True)
    i2 = jnp.min(jnp.where(rest == v2, lane, 128), axis=-1, keepdims=True)
    e = jnp.exp(v2 - v1)
    w1 = 1.0 / (1.0 + e)
    w2 = e / (1.0 + e)
    out_lane = lax.broadcasted_iota(jnp.int32, (TM, 8), 1)
    r = jnp.where(out_lane == 0, i1.astype(F32), 0.0)
    r = jnp.where(out_lane == 1, i2.astype(F32), r)
    r = jnp.where(out_lane == 2, w1, r)
    r = jnp.where(out_lane == 3, w2, r)
    r_out[...] = r


RUN_ALIGN = 8
RUN_BITS = tuple(1 << b for b in range(TM.bit_length() - 1, RUN_ALIGN.bit_length() - 2, -1))
LOCAL_ROWS = 2 * TM + 64


def _run_copies(run_ref, local, remote, sem, *, to_remote, wait):
    for e in range(N_EXPERTS):
        l0, n, g0 = run_ref[0, 0, e], run_ref[0, 1, e], run_ref[0, 2, e]
        done = jnp.int32(0)
        for bit in RUN_BITS:
            take = (n & bit) != 0

            @pl.when(take)
            def _(done=done, bit=bit):
                loc = local.at[pl.ds(pl.multiple_of(l0 + done, RUN_ALIGN), bit)]
                rem = remote.at[pl.ds(pl.multiple_of(g0 + done, RUN_ALIGN), bit)]
                cp = pltpu.make_async_copy(loc, rem, sem) if to_remote else pltpu.make_async_copy(rem, loc, sem)
                if wait:
                    cp.wait()
                else:
                    cp.start()

            done = done + (n & bit)


def _block_onehot(lp, shape, axis):
    return jnp.where(lax.broadcasted_iota(jnp.int32, shape, axis) == lp, 1.0, 0.0).astype(BF16)


def _dispatch_kernel(run_ref, prev_ref, x_ref, mod_ref, g_ref, lp_ref, init_hbm, xs_hbm, sbuf, sems):
    del init_hbm
    t = pl.program_id(0)
    slot = lax.rem(t, 2)
    h = _modulate(x_ref[...], g_ref[...], mod_ref[3:4, :], mod_ref[4:5, :]).astype(BF16)
    lp = lp_ref[...]
    place = _block_onehot(lp[0:1, :], (LOCAL_ROWS, TM), 0) + _block_onehot(lp[1:2, :], (LOCAL_ROWS, TM), 0)
    sbuf[slot] = jnp.dot(place, h, preferred_element_type=F32)

    @pl.when(t > 0)
    def _():
        _run_copies(prev_ref, sbuf.at[1 - slot], xs_hbm, sems.at[1 - slot], to_remote=True, wait=True)

    _run_copies(run_ref, sbuf.at[slot], xs_hbm, sems.at[slot], to_remote=True, wait=False)

    @pl.when(t == pl.num_programs(0) - 1)
    def _():
        _run_copies(run_ref, sbuf.at[slot], xs_hbm, sems.at[slot], to_remote=True, wait=True)


def _expert_kernel(meta_ref, xs_ref, wgu_ref, wd_ref, y_ref):
    t = pl.program_id(0)
    n_used = meta_ref[pl.num_programs(0)]

    @pl.when(t < n_used)
    def _():
        y_ref[...] = _swiglu_rows(xs_ref[...].astype(BF16), wgu_ref, wd_ref)

    @pl.when(t >= n_used)
    def _():
        y_ref[...] = jnp.zeros_like(y_ref)


def _combine_kernel(run_ref, next_ref, y_hbm, x_ref, mod_ref, r_ref, lp_ref, fin_ref, o_ref, ybuf, sems, *, final):
    t = pl.program_id(0)
    slot = lax.rem(t, 2)

    def fetch(table, s):
        ybuf[s, 2 * TM:LOCAL_ROWS, :] = jnp.zeros((LOCAL_ROWS - 2 * TM, D), F32)
        _run_copies(table, ybuf.at[s], y_hbm, sems.at[s], to_remote=False, wait=False)

    @pl.when(t == 0)
    def _():
        fetch(run_ref, 0)

    @pl.when(t + 1 < pl.num_programs(0))
    def _():
        fetch(next_ref, 1 - slot)

    _run_copies(run_ref, ybuf.at[slot], y_hbm, sems.at[slot], to_remote=False, wait=True)
    ys = ybuf[slot].astype(BF16)
    lp = lp_ref[...]
    y0 = jnp.dot(_block_onehot(lp[:, 0:1], (TM, LOCAL_ROWS), 1), ys, preferred_element_type=F32)
    y1 = jnp.dot(_block_onehot(lp[:, 1:2], (TM, LOCAL_ROWS), 1), ys, preferred_element_type=F32)
    r = r_ref[...]
    x = x_ref[...] + mod_ref[5:6, :] * (r[:, 2:3] * y0 + r[:, 3:4] * y1)
    if final:
        x = (x * lax.rsqrt(jnp.mean(x * x, axis=-1, keepdims=True) + EPS)) * fin_ref[...]
    o_ref[...] = x


def _moe_ffn(x, mods, norm_g, w_router, w_gu, w_down, n_ctx, final_g=None):
    B, T, _ = x.shape
    nt = T // TM
    nct = n_ctx // TM
    n_tok = B * T
    nblk = B * nt
    wr = jnp.zeros((D, 128), F32).at[:, :N_EXPERTS].set(w_router)
    wr_hi = wr.astype(BF16)
    route = pl.pallas_call(
        _router_kernel,
        out_shape=_sds((B, T, 8), F32),
        grid=(B, nt),
        in_specs=[_x_spec(), _mod_spec(nct), _const_spec((1, D)), _const_spec((D, 128)), _const_spec((D, 128))],
        out_specs=pl.BlockSpec((None, TM, 8), lambda b, i: (b, i, 0)),
        compiler_params=_params(("parallel", "parallel"), 32),
        name="router",
    )(x, mods, norm_g.reshape(1, D), wr_hi, (wr - wr_hi.astype(F32)).astype(BF16))

    ids = jnp.arange(N_EXPERTS, dtype=jnp.int32)
    eid = route.reshape(nblk, TM, 8)[:, :, :2].astype(jnp.int32).reshape(nblk, 2 * TM)
    onehot = (eid[:, :, None] == ids).astype(jnp.int32)
    cnt = jnp.sum(onehot, axis=1)
    cnt = ((cnt + RUN_ALIGN - 1) // RUN_ALIGN) * RUN_ALIGN
    l0 = jnp.cumsum(cnt, axis=1) - cnt
    rank = jnp.sum((jnp.cumsum(onehot, axis=1) - onehot) * onehot, axis=2)
    lp = jnp.sum(onehot * l0[:, None, :], axis=2) + rank
    counts = jnp.sum(cnt, axis=0)
    padded = ((counts + TME - 1) // TME) * TME
    ends = jnp.cumsum(padded)
    g0 = (ends - padded)[None, :] + jnp.cumsum(cnt, axis=0) - cnt
    runs = jnp.stack([l0, cnt, g0], axis=1)
    lp_cols = lp.reshape(nblk, TM, 2)
    lp_rows = lp_cols.transpose(0, 2, 1)
    n_tiles = -(-(2 * n_tok + nblk * N_EXPERTS * (RUN_ALIGN - 1) + N_EXPERTS * (TME - 1)) // TME)
    n_rows = n_tiles * TME
    tile_row = jnp.arange(n_tiles, dtype=jnp.int32)[:, None] * TME
    tile_exp = jnp.minimum(jnp.sum((ends[None, :] <= tile_row).astype(jnp.int32), axis=1), N_EXPERTS - 1)
    meta = jnp.concatenate([tile_exp, (ends[-1:] // TME).astype(jnp.int32)])

    def specs(nto, off):
        blk = lambda t: (t // nto) * nt + t % nto + off
        run = lambda shift, last: pl.BlockSpec(
            (1, 3, N_EXPERTS), lambda t: (blk(jnp.clip(t + shift, 0, last)), 0, 0), memory_space=pltpu.SMEM)
        tok = lambda width: pl.BlockSpec((None, TM, width), lambda t: (t // nto, t % nto + off, 0))
        mod = pl.BlockSpec((None, None, 6, D), lambda t: (t // nto, jnp.where(t % nto + off >= nct, 1, 0), 0, 0))
        return blk, run, tok, mod

    blk, run, tok, mod = specs(nt, 0)
    xs = pl.pallas_call(
        _dispatch_kernel,
        out_shape=_sds((n_rows, D), F32),
        grid=(nblk,),
        in_specs=[run(0, nblk - 1), run(-1, nblk - 1), tok(D), mod, pl.BlockSpec((1, D), lambda t: (0, 0)),
                  pl.BlockSpec((None, 2, TM), lambda t: (blk(t), 0, 0)),
                  pl.BlockSpec(memory_space=pl.ANY)],
        out_specs=pl.BlockSpec(memory_space=pl.ANY),
        scratch_shapes=[pltpu.VMEM((2, LOCAL_ROWS, D), F32), pltpu.SemaphoreType.DMA((2,))],
        input_output_aliases={6: 0},
        compiler_params=_params(("arbitrary",), 40),
        name="moe_dispatch",
    )(runs, runs, x, mods, norm_g.reshape(1, D), lp_rows, jnp.zeros((n_rows, D), F32))

    y = pl.pallas_call(
        _expert_kernel,
        out_shape=_sds((n_rows, D), F32),
        grid_spec=pltpu.PrefetchScalarGridSpec(
            num_scalar_prefetch=1,
            grid=(n_tiles,),
            in_specs=[pl.BlockSpec((TME, D), lambda t, meta: (t, 0)),
                      pl.BlockSpec((None, D, 2 * D_FF), lambda t, meta: (meta[t], 0, 0)),
                      pl.BlockSpec((None, D_FF, D), lambda t, meta: (meta[t], 0, 0))],
            out_specs=pl.BlockSpec((TME, D), lambda t, meta: (t, 0))),
        compiler_params=_params(("arbitrary",), 56),
        name="experts",
    )(meta, xs, w_gu.astype(BF16), w_down.astype(BF16))

    final = final_g is not None
    off = nct if final else 0
    nto = nt - off
    n_steps = B * nto
    blk, run, tok, mod = specs(nto, off)
    return pl.pallas_call(
        functools.partial(_combine_kernel, final=final),
        out_shape=_sds((B, nto * TM, D), F32),
        grid=(n_steps,),
        in_specs=[run(0, n_steps - 1), run(1, n_steps - 1), pl.BlockSpec(memory_space=pl.ANY), tok(D), mod, tok(8),
                  pl.BlockSpec((None, TM, 2), lambda t: (blk(t), 0, 0)),
                  pl.BlockSpec((1, D), lambda t: (0, 0))],
        out_specs=pl.BlockSpec((None, TM, D), lambda t: (t // nto, t % nto, 0)),
        scratch_shapes=[pltpu.VMEM((2, LOCAL_ROWS, D), F32), pltpu.SemaphoreType.DMA((2,))],
        compiler_params=_params(("arbitrary",), 40),
        name="moe_combine",
    )(runs, runs, y, x, mods, route, lp_cols, (final_g if final else norm_g).reshape(1, D))


def kernel(x, c, ctx, c_ctx, l0_ada_w, l0_ada_b, l0_norm_mix, l0_norm_ffn, l0_conv_w_in, l0_conv_w, l0_conv_w_out, l0_ffn_w_gu, l0_ffn_w_down, l1_ada_w, l1_ada_b, l1_norm_mix, l1_norm_ffn, l1_attn_w_qkv, l1_attn_lam, l1_attn_subln, l1_attn_w_o, l1_moe_router, l1_moe_w_gu, l1_moe_w_down, l2_ada_w, l2_ada_b, l2_norm_mix, l2_norm_ffn, l2_ssm_a_re, l2_ssm_a_im, l2_ssm_log_dt, l2_ssm_b_re, l2_ssm_b_im, l2_ssm_c_re, l2_ssm_c_im, l2_ssm_d, l2_ssm_w_glu, l2_ffn_w_gu, l2_ffn_w_down, l3_ada_w, l3_ada_b, l3_norm_mix, l3_norm_ffn, l3_conv_w_in, l3_conv_w, l3_conv_w_out, l3_moe_router, l3_moe_w_gu, l3_moe_w_down, final_norm):
    B, n_lat, _ = x.shape
    n_ctx = ctx.shape[1]
    T = n_ctx + n_lat
    assert n_ctx % TM == 0 and n_lat % TM == 0 and n_lat % GRID_W == 0 and B < 8 and B % 2 == 0
    xs = jnp.concatenate([ctx, x], axis=1)
    cond8 = jnp.concatenate([c, c_ctx[None], jnp.zeros((8 - B - 1, D), F32)], axis=0)

    def mods_of(w, b):
        m = _ada(cond8, w, b).reshape(8, 6, D)
        return jnp.stack([jnp.broadcast_to(m[B], (B, 6, D)), m[:B]], axis=1)

    mods = mods_of(l0_ada_w, l0_ada_b)
    xs = _conv_mixer(xs, mods, l0_norm_mix, l0_conv_w_in, l0_conv_w, l0_conv_w_out, n_ctx)
    xs = _dense_ffn(xs, mods, l0_norm_ffn, l0_ffn_w_gu, l0_ffn_w_down, n_ctx)

    mods = mods_of(l1_ada_w, l1_ada_b)
    xs = _attn_mixer(xs, mods, l1_norm_mix, l1_attn_w_qkv, l1_attn_lam, l1_attn_subln, l1_attn_w_o, n_ctx,
                     0.8 - 0.6 * math.exp(-0.3 * 1))
    xs = _moe_ffn(xs, mods, l1_norm_ffn, l1_moe_router, l1_moe_w_gu, l1_moe_w_down, n_ctx)

    mods = mods_of(l2_ada_w, l2_ada_b)
    xs = _s5_mixer(xs, mods, l2_norm_mix,
                   (l2_ssm_a_re, l2_ssm_a_im, l2_ssm_log_dt, l2_ssm_b_re, l2_ssm_b_im, l2_ssm_c_re, l2_ssm_c_im,
                    l2_ssm_d, l2_ssm_w_glu), n_ctx)
    xs = _dense_ffn(xs, mods, l2_norm_ffn, l2_ffn_w_gu, l2_ffn_w_down, n_ctx)

    mods = mods_of(l3_ada_w, l3_ada_b)
    xs = _conv_mixer(xs, mods, l3_norm_mix, l3_conv_w_in, l3_conv_w, l3_conv_w_out, n_ctx)
    return _moe_ffn(xs, mods, l3_norm_ffn, l3_moe_router, l3_moe_w_gu, l3_moe_w_down, n_ctx, final_g=final_norm)
```

```python
import functools
import math

import jax
import jax.numpy as jnp
from jax import lax
from jax.experimental import pallas as pl
from jax.experimental.pallas import tpu as pltpu

F32 = jnp.float32
BF16 = jnp.bfloat16
HIGHEST = lax.Precision.HIGHEST

D = 1024
GRID_W = 64
N_HEADS = 8
HEAD_DIM = 64
V_DIM = 128
ROPE_BASE = 10000.0
GROUP_SIZE = 16
N_GROUPS = 64
STATE_DIM = 64
D_FF = 2816
N_EXPERTS = 8
EPS = 1e-6

TM = 256
FF_CHUNK = 256
S5_T = 16
TME = 512
MIB = 1 << 20


def _sds(shape, dtype):
    return jax.ShapeDtypeStruct(shape, dtype)


def _params(sem, vmem_mib):
    return pltpu.CompilerParams(dimension_semantics=sem, vmem_limit_bytes=vmem_mib * MIB)


def _modulate(x, g, shift, scale):
    y = x * lax.rsqrt(jnp.mean(x * x, axis=-1, keepdims=True) + EPS)
    return (y * g) * (1.0 + scale) + shift


def _x_spec():
    return pl.BlockSpec((None, TM, D), lambda b, i: (b, i, 0))


def _mod_spec(n_ctx_tiles):
    return pl.BlockSpec((None, None, 6, D), lambda b, i: (b, jnp.where(i >= n_ctx_tiles, 1, 0), 0, 0))


def _const_spec(shape):
    nd = len(shape)
    return pl.BlockSpec(shape, lambda b, i: (0,) * nd)


def _pair_spec():
    return pl.BlockSpec((2, TM, D), lambda b, i: (b, i, 0))


def _pair_mod_spec(n_ctx_tiles):
    return pl.BlockSpec((2, None, 6, D), lambda b, i: (b, jnp.where(i >= n_ctx_tiles, 1, 0), 0, 0))


def _modulate_pair(x_ref, g_ref, mod_ref, shift_row, scale_row):
    return jnp.concatenate(
        [_modulate(x_ref[j], g_ref[...], mod_ref[j, shift_row:shift_row + 1, :], mod_ref[j, scale_row:scale_row + 1, :])
         for j in range(2)], axis=0).astype(BF16)


def _ada_kernel(c_ref, w_ref, b_ref, o_ref):
    c = c_ref[...]
    s = c * jax.nn.sigmoid(c)
    o_ref[...] = jnp.dot(s, w_ref[...], preferred_element_type=F32, precision=HIGHEST) + b_ref[...]


def _ada(cond8, w, b):
    nb = 4
    bn = w.shape[1] // nb
    return pl.pallas_call(
        _ada_kernel,
        out_shape=_sds((8, w.shape[1]), F32),
        grid=(nb,),
        in_specs=[pl.BlockSpec((8, D), lambda j: (0, 0)),
                  pl.BlockSpec((D, bn), lambda j: (0, j)),
                  pl.BlockSpec((1, bn), lambda j: (0, j))],
        out_specs=pl.BlockSpec((8, bn), lambda j: (0, j)),
        compiler_params=_params(("parallel",), 40),
        name="ada",
    )(cond8, w, b.reshape(1, -1))


def _conv_in_kernel(x_ref, mod_ref, g_ref, w_ref, b_out, z_out):
    h = _modulate_pair(x_ref, g_ref, mod_ref, 0, 1)
    b_gate = jnp.dot(h, w_ref[:, 0:D], preferred_element_type=F32).astype(BF16)
    c_gate = jnp.dot(h, w_ref[:, D:2 * D], preferred_element_type=F32)
    v = jnp.dot(h, w_ref[:, 2 * D:3 * D], preferred_element_type=F32)
    z = c_gate * v
    for j in range(2):
        b_out[j] = b_gate[j * TM:(j + 1) * TM, :]
        z_out[j] = z[j * TM:(j + 1) * TM, :]


def _conv_out_kernel(z_ref, zp_ref, zn_ref, b_ref, cw_ref, w_ref, x_ref, mod_ref, o_ref, *, n_ctx, n_tot):
    i = pl.program_id(1)
    z = z_ref[...]
    row = lax.broadcasted_iota(jnp.int32, (TM, 1), 0)
    starts_seq = (i == 0) | (i == n_ctx // TM)
    ends_seq = (i == n_ctx // TM - 1) | (i == n_tot // TM - 1)
    prev_row = jnp.where(starts_seq, 0.0, zp_ref[7:8, :])
    next_row = jnp.where(ends_seq, 0.0, zn_ref[0:1, :])
    zm1 = jnp.where(row == 0, prev_row, pltpu.roll(z, 1, axis=0))
    zp1 = jnp.where(row == TM - 1, next_row, pltpu.roll(z, TM - 1, axis=0))
    y = cw_ref[0:1, :] * zm1 + cw_ref[1:2, :] * z + cw_ref[2:3, :] * zp1
    a = (b_ref[...].astype(F32) * y).astype(BF16)
    o_ref[...] = x_ref[...] + mod_ref[2:3, :] * jnp.dot(a, w_ref[...], preferred_element_type=F32)


def _conv_mixer(x, mods, norm_g, w_in, conv_w, w_out, n_ctx):
    B, T, _ = x.shape
    nt = T // TM
    nct = n_ctx // TM
    b_gate, z = pl.pallas_call(
        _conv_in_kernel,
        out_shape=(_sds((B, T, D), BF16), _sds((B, T, D), F32)),
        grid=(B // 2, nt),
        in_specs=[_pair_spec(), _pair_mod_spec(nct), _const_spec((1, D)), _const_spec((D, 3 * D))],
        out_specs=(_pair_spec(), _pair_spec()),
        compiler_params=_params(("parallel", "parallel"), 48),
        name="conv_in",
    )(x, mods, norm_g.reshape(1, D), w_in.astype(BF16))
    r8 = TM // 8
    return pl.pallas_call(
        functools.partial(_conv_out_kernel, n_ctx=n_ctx, n_tot=T),
        out_shape=_sds((B, T, D), F32),
        grid=(B, nt),
        in_specs=[_x_spec(),
                  pl.BlockSpec((None, 8, D), lambda b, i: (b, jnp.maximum(i * r8 - 1, 0), 0)),
                  pl.BlockSpec((None, 8, D), lambda b, i: (b, jnp.minimum((i + 1) * r8, T // 8 - 1), 0)),
                  _x_spec(), _const_spec((3, D)), _const_spec((D, D)), _x_spec(), _mod_spec(nct)],
        out_specs=_x_spec(),
        compiler_params=_params(("parallel", "parallel"), 48),
        name="conv_out",
    )(z, z, z, b_gate, conv_w, w_out.astype(BF16), x, mods)


def _qkv_kernel(x_ref, mod_ref, g_ref, w_ref, cos_ref, slo_ref, shi_ref, qt_out, k_out, vt_out):
    h = _modulate_pair(x_ref, g_ref, mod_ref, 0, 1)
    twice = lambda tab: jnp.concatenate([tab[...], tab[...]], axis=0)
    cos, s_lo, s_hi = twice(cos_ref), twice(slo_ref), twice(shi_ref)

    def rope(t):
        return t * cos + pltpu.roll(t, 128 - 16, axis=1) * s_lo + pltpu.roll(t, 16, axis=1) * s_hi

    q_scale = (HEAD_DIM ** -0.5) * math.log2(math.e)
    for hd in range(N_HEADS):
        lo, hi = hd * 128, (hd + 1) * 128
        q = rope(jnp.dot(h, w_ref[:, lo:hi], preferred_element_type=F32) * q_scale)
        k = rope(jnp.dot(h, w_ref[:, D + lo:D + hi], preferred_element_type=F32)).astype(BF16)
        v = jnp.dot(h, w_ref[:, 2 * D + lo:2 * D + hi], preferred_element_type=F32).astype(BF16).astype(F32)
        for j in range(2):
            rows = slice(j * TM, (j + 1) * TM)
            qt_out[j, hd] = q[rows, :].T.astype(BF16)
            k_out[j, :, lo:hi] = k[rows, :]
            vt_out[j, hd] = v[rows, :].T.astype(BF16)


def _attn_kernel(qt_ref, k_ref, vt_ref, lam_ref, sub_ref, o_ref, s_buf, acc_ref, *,
                 n_ctx_tiles, n_lat_chunks, lam_init):
    i = pl.program_id(2)
    qt = qt_ref[...]
    row = lax.broadcasted_iota(jnp.int32, qt.shape, 0)
    zero = jnp.zeros_like(qt)
    q_cat = jnp.concatenate([jnp.where(row < HEAD_DIM, qt, zero), jnp.where(row >= HEAD_DIM, qt, zero)], axis=1)

    def scores(first_tile, n_tiles):
        start = first_tile * TM
        if not isinstance(first_tile, int):
            start = pl.multiple_of(start, TM)
        s = jnp.dot(k_ref[pl.ds(start, n_tiles * TM), :], q_cat, preferred_element_type=F32)
        return s, jnp.max(s, axis=0, keepdims=True)

    def absorb(s, s_max, first_tile, n_tiles, m, l):
        m_new = jnp.maximum(m, s_max)
        p = jnp.exp2(s - m_new)
        alpha = jnp.exp2(m - m_new)
        l_new = alpha * l + jnp.sum(p, axis=0, keepdims=True)
        pb = p.astype(BF16)
        pv = jnp.dot(vt_ref[first_tile], pb[0:TM, :], preferred_element_type=F32)
        for c in range(1, n_tiles):
            pv = pv + jnp.dot(vt_ref[first_tile + c], pb[c * TM:(c + 1) * TM, :], preferred_element_type=F32)
        acc_ref[...] = alpha * acc_ref[...] + pv
        return m_new, l_new

    def finish(l):
        acc = acc_ref[...]
        lv = lam_ref[...]
        lam = (jnp.exp(jnp.sum(lv[0:1, :] * lv[1:2, :], axis=-1, keepdims=True))
               - jnp.exp(jnp.sum(lv[2:3, :] * lv[3:4, :], axis=-1, keepdims=True)) + lam_init)
        o = (acc[:, 0:TM] / l[:, 0:TM] - lam * (acc[:, TM:2 * TM] / l[:, TM:2 * TM])).T
        o = o * lax.rsqrt(jnp.mean(o * o, axis=-1, keepdims=True) + EPS)
        o_ref[...] = ((o * sub_ref[...]) * (1.0 - lam_init)).astype(BF16)

    acc_ref[...] = jnp.zeros_like(acc_ref)
    s_ctx, max_ctx = scores(0, n_ctx_tiles)
    s_buf[0], max_first = scores(n_ctx_tiles, 2)
    m_ctx, l_ctx = absorb(s_ctx, max_ctx, 0, n_ctx_tiles,
                          jnp.full((1, 2 * TM), -1e30, F32), jnp.zeros((1, 2 * TM), F32))

    @pl.when(i < n_ctx_tiles)
    def _():
        finish(l_ctx)

    @pl.when(i >= n_ctx_tiles)
    def _():
        nbuf = s_buf.shape[0]
        m, l, maxes = m_ctx, l_ctx, [max_first] + [None] * (nbuf - 1)
        for c in range(n_lat_chunks):
            if c + 1 < n_lat_chunks:
                s_buf[(c + 1) % nbuf], maxes[(c + 1) % nbuf] = scores(n_ctx_tiles + 2 * (c + 1), 2)
            m, l = absorb(s_buf[c % nbuf], maxes[c % nbuf], n_ctx_tiles + 2 * c, 2, m, l)
        finish(l)


def _out_proj_kernel(a_ref, w_ref, x_ref, mod_ref, o_ref):
    a = jnp.concatenate([a_ref[0], a_ref[1]], axis=0)
    y = jnp.dot(a, w_ref[...], preferred_element_type=F32)
    for j in range(2):
        o_ref[j] = x_ref[j] + mod_ref[j, 2:3, :] * y[j * TM:(j + 1) * TM, :]


def _rope_tables(n_ctx, n_lat):
    t = jnp.arange(n_lat, dtype=jnp.int32)
    row = (t // GRID_W).astype(F32)
    col = (t % GRID_W).astype(F32)
    n_freq = HEAD_DIM // 4
    inv = ROPE_BASE ** (-jnp.arange(n_freq, dtype=F32) / n_freq)
    ang_r = row[:, None] * inv
    ang_c = col[:, None] * inv
    zero = jnp.zeros_like(ang_r)
    cos64 = jnp.concatenate([jnp.cos(ang_r), jnp.cos(ang_r), jnp.cos(ang_c), jnp.cos(ang_c)], axis=-1)
    slo64 = jnp.concatenate([-jnp.sin(ang_r), zero, -jnp.sin(ang_c), zero], axis=-1)
    shi64 = jnp.concatenate([zero, jnp.sin(ang_r), zero, jnp.sin(ang_c)], axis=-1)

    def full(tab, ctx_val):
        tab = jnp.concatenate([tab, tab], axis=-1)
        return jnp.concatenate([jnp.full((n_ctx, 128), ctx_val, F32), tab], axis=0)

    return full(cos64, 1.0), full(slo64, 0.0), full(shi64, 0.0)


def _attn_mixer(x, mods, norm_g, w_qkv, lam_vecs, subln, w_o, n_ctx, lam_init):
    B, T, _ = x.shape
    nt = T // TM
    nct = n_ctx // TM
    cos, s_lo, s_hi = _rope_tables(n_ctx, T - n_ctx)
    tab_spec = pl.BlockSpec((TM, 128), lambda b, i: (i, 0))
    assert (T - n_ctx) % (4 * TM) == 0
    t_spec = pl.BlockSpec((2, N_HEADS, None, 128, TM), lambda b, i: (b, 0, i, 0, 0))
    t_shape = _sds((B, N_HEADS, nt, 128, TM), BF16)
    qt, k, vt = pl.pallas_call(
        _qkv_kernel,
        out_shape=(t_shape, _sds((B, T, D), BF16), t_shape),
        grid=(B // 2, nt),
        in_specs=[_pair_spec(), _pair_mod_spec(nct), _const_spec((1, D)), _const_spec((D, 3 * D)),
                  tab_spec, tab_spec, tab_spec],
        out_specs=(t_spec, _pair_spec(), t_spec),
        compiler_params=_params(("parallel", "parallel"), 48),
        name="qkv",
    )(x, mods, norm_g.reshape(1, D), w_qkv.astype(BF16), cos, s_lo, s_hi)

    attn = pl.pallas_call(
        functools.partial(_attn_kernel, n_ctx_tiles=nct, n_lat_chunks=(T - n_ctx) // (2 * TM), lam_init=lam_init),
        out_shape=_sds((B, T, D), BF16),
        grid=(B, N_HEADS, nt),
        in_specs=[pl.BlockSpec((None, None, None, 128, TM), lambda b, h, i: (b, h, i, 0, 0)),
                  pl.BlockSpec((None, T, 128), lambda b, h, i: (b, 0, h)),
                  pl.BlockSpec((None, None, nt, 128, TM), lambda b, h, i: (b, h, 0, 0, 0)),
                  pl.BlockSpec((4, HEAD_DIM), lambda b, h, i: (0, 0)),
                  pl.BlockSpec((1, V_DIM), lambda b, h, i: (0, 0))],
        out_specs=pl.BlockSpec((None, TM, 128), lambda b, h, i: (b, i, h)),
        scratch_shapes=[pltpu.VMEM((2, 2 * TM, 2 * TM), F32), pltpu.VMEM((V_DIM, 2 * TM), F32)],
        compiler_params=_params(("parallel", "parallel", "parallel"), 48),
        name="diff_attn",
    )(qt, k, vt, lam_vecs, subln.reshape(1, V_DIM))

    return pl.pallas_call(
        _out_proj_kernel,
        out_shape=_sds((B, T, D), F32),
        grid=(B // 2, nt),
        in_specs=[_pair_spec(), _const_spec((D, D)), _pair_spec(), _pair_mod_spec(nct)],
        out_specs=_pair_spec(),
        compiler_params=_params(("parallel", "parallel"), 48),
        name="attn_out",
    )(attn, w_o.astype(BF16), x, mods)


def _mod_only_kernel(x_ref, mod_ref, g_ref, h_out):
    h_out[...] = _modulate(x_ref[...], g_ref[...], mod_ref[0:1, :], mod_ref[1:2, :])


def _s5_chunk_rows(h_ref):
    return jnp.concatenate([h_ref[:, s, :] for s in range(S5_T)], axis=-1).astype(BF16)


def _s5_state_kernel(h_ref, w_ref, sfr, sfi, sbr, sbi):
    s = jnp.dot(_s5_chunk_rows(h_ref), w_ref[...], preferred_element_type=F32)
    sfr[...] = s[:, 0:512]
    sfi[...] = s[:, 512:1024]
    sbr[...] = s[:, 1024:1536]
    sbi[...] = s[:, 1536:2048]


def _s5_scan_kernel(sfr, sfi, sbr, sbi, afr, afi, abr, abi, hfr, hfi, hbr, hbi, *, n_ctx_chunks, n_chunks):
    ar, ai = afr[...], afi[...]

    def fwd(n, carry):
        hr, hi = carry
        hfr[n] = hr
        hfi[n] = hi
        sr, si = sfr[n], sfi[n]
        return ar * hr - ai * hi + sr, ar * hi + ai * hr + si

    zero = jnp.zeros(ar.shape, F32)
    lax.fori_loop(0, n_chunks, fwd, (zero, zero))

    br, bi = abr[...], abi[...]

    def bwd(t, carry, top):
        n = top - t
        hr, hi = carry
        hbr[n] = hr
        hbi[n] = hi
        sr, si = sbr[n], sbi[n]
        return br * hr - bi * hi + sr, br * hi + bi * hr + si

    carry = lax.fori_loop(0, n_ctx_chunks, functools.partial(bwd, top=n_ctx_chunks - 1), (zero, zero))
    lax.fori_loop(0, n_chunks - n_ctx_chunks, functools.partial(bwd, top=n_chunks - 1), carry)


def _s5_out_kernel(h_ref, m_ref, hfr, hfi, hbr, hbi, v_ref, y_ref):
    state = jnp.concatenate([hfr[...], hfi[...], hbr[...], hbi[...]], axis=-1).astype(BF16)
    y = (jnp.dot(_s5_chunk_rows(h_ref), m_ref[...], preferred_element_type=F32)
         + jnp.dot(state, v_ref[...], preferred_element_type=F32))
    for t in range(S5_T // 2):
        y_ref[:, t, :] = y[:, t * 128:(t + 1) * 128]


def _glu_kernel(y_ref, x_ref, mod_ref, g_ref, d_ref, w_ref, o_ref):
    def act(j):
        hl = _modulate(x_ref[j], g_ref[...], mod_ref[j, 0:1, :], mod_ref[j, 1:2, :])
        return jax.nn.gelu(y_ref[j] + d_ref[...] * hl).astype(BF16)

    a = jnp.concatenate([act(0), act(1)], axis=0)
    val = jnp.dot(a, w_ref[:, 0:D], preferred_element_type=F32)
    gate = jnp.dot(a, w_ref[:, D:2 * D], preferred_element_type=F32)
    out = val * jax.nn.sigmoid(gate)
    for j in range(2):
        o_ref[j] = x_ref[j] + mod_ref[j, 2:3, :] * out[j * TM:(j + 1) * TM, :]


def _s5_matrices(a_re, a_im, log_dt, b_re, b_im, c_re, c_im):
    T = S5_T
    hp = functools.partial(jnp.einsum, precision=HIGHEST)
    pw_re, pw_im, bb_re, bb_im = [], [], [], []
    for d in range(2):
        ar, ai = a_re[d].astype(F32), a_im[d].astype(F32)
        dt = jnp.exp(log_dt[d].astype(F32))[:, None]
        mag = jnp.exp(dt * ar)
        ab_re = mag * jnp.cos(dt * ai)
        ab_im = mag * jnp.sin(dt * ai)
        den = ar * ar + ai * ai
        nr = ab_re - 1.0
        co_re = (nr * ar + ab_im * ai) / den
        co_im = (ab_im * ar - nr * ai) / den
        br, bi = b_re[d].astype(F32), b_im[d].astype(F32)
        bb_re.append(co_re[..., None] * br - co_im[..., None] * bi)
        bb_im.append(co_re[..., None] * bi + co_im[..., None] * br)
        pr = [jnp.ones_like(ab_re)]
        pi = [jnp.zeros_like(ab_re)]
        for _ in range(T):
            pr.append(pr[-1] * ab_re - pi[-1] * ab_im)
            pi.append(pr[-2] * ab_im + pi[-1] * ab_re)
        pw_re.append(jnp.stack(pr))
        pw_im.append(jnp.stack(pi))

    def lag_kernels(d):
        mr = pw_re[d][:T, :, :, None] * bb_re[d][None] - pw_im[d][:T, :, :, None] * bb_im[d][None]
        mi = pw_re[d][:T, :, :, None] * bb_im[d][None] + pw_im[d][:T, :, :, None] * bb_re[d][None]
        return hp('gop,kgpi->kgoi', c_re[d].astype(F32), mr) - hp('gop,kgpi->kgoi', c_im[d].astype(F32), mi)

    kf, kb = lag_kernels(0), lag_kernels(1)
    lag0 = (kf[0] + kb[0])[None]
    kfull = jnp.concatenate([kb[:0:-1], lag0, kf[1:]], axis=0)
    tt = jnp.arange(T)
    idx = tt[None, :] - tt[:, None] + (T - 1)
    mt = kfull[idx]
    mt = mt.transpose(2, 0, 4, 1, 3).reshape(N_GROUPS, T * GROUP_SIZE, T * GROUP_SIZE)

    def end_state(d, powers):
        pr, pi = pw_re[d][powers], pw_im[d][powers]
        wr = pr[..., None] * bb_re[d][None] - pi[..., None] * bb_im[d][None]
        wi = pr[..., None] * bb_im[d][None] + pi[..., None] * bb_re[d][None]
        f = lambda w: w.transpose(1, 0, 3, 2).reshape(N_GROUPS, T * GROUP_SIZE, STATE_DIM)
        return f(wr), f(wi)

    def read_out(d, powers):
        pr, pi = pw_re[d][powers], pw_im[d][powers]
        cr, ci = c_re[d].astype(F32), c_im[d].astype(F32)
        qr = cr[None] * pr[:, :, None, :] - ci[None] * pi[:, :, None, :]
        qi = cr[None] * pi[:, :, None, :] + ci[None] * pr[:, :, None, :]
        f = lambda q: q.transpose(1, 3, 0, 2).reshape(N_GROUPS, STATE_DIM, T * GROUP_SIZE)
        return f(qr), f(-qi)

    wfr, wfi = end_state(0, T - 1 - tt)
    wbr, wbi = end_state(1, tt)
    vfr, vfi = read_out(0, tt + 1)
    vbr, vbi = read_out(1, T - tt)

    gsz = T * GROUP_SIZE
    g8 = jnp.arange(8, dtype=jnp.int32)[:, None]
    k = jnp.arange(gsz, dtype=jnp.int32)[None, :]
    chunk_col = (k // GROUP_SIZE) * 128 + g8 * GROUP_SIZE + k % GROUP_SIZE
    state_col = (k // STATE_DIM) * (8 * STATE_DIM) + g8 * STATE_DIM + k % STATE_DIM
    lanes = jnp.arange(8 * gsz, dtype=jnp.int32)[None, None, :]
    to_chunk = (chunk_col[:, :, None] == lanes).astype(BF16)
    to_state = (state_col[:, :, None] == lanes).astype(BF16)
    w_cat = jnp.stack([wfr, wfi, wbr, wbi], axis=2)
    v_cat = jnp.stack([vfr, vfi, vbr, vbi], axis=1)
    m8 = _spread(mt, to_chunk, T, GROUP_SIZE)
    w8 = _spread(w_cat.reshape(N_GROUPS, gsz, gsz), to_state, T, GROUP_SIZE)
    v8 = _spread(v_cat.reshape(N_GROUPS, gsz, gsz), to_chunk, 4, STATE_DIM)
    a_pow = [t[T].reshape(N_GROUPS // 2, 2 * STATE_DIM) for t in (pw_re[0], pw_im[0], pw_re[1], pw_im[1])]
    return m8, w8, v8, a_pow


def _spread_kernel(a_ref, c_ref, o_ref):
    e = jnp.dot(a_ref[...], c_ref[...], preferred_element_type=F32)
    o_ref[...] = e.reshape(o_ref.shape).astype(BF16)


def _spread(a, col_of, r1, r2):
    nb = N_GROUPS // 8
    kdim, n = col_of.shape[1], col_of.shape[2]
    out = pl.pallas_call(
        _spread_kernel,
        out_shape=_sds((nb, r1, 8, r2, n), BF16),
        grid=(nb, 8),
        in_specs=[pl.BlockSpec((None, None, r1 * r2, kdim), lambda g, j: (g, j, 0, 0)),
                  pl.BlockSpec((None, kdim, n), lambda g, j: (j, 0, 0))],
        out_specs=pl.BlockSpec((None, r1, None, r2, n), lambda g, j: (g, 0, j, 0, 0)),
        compiler_params=_params(("parallel", "parallel"), 32),
        name="s5_spread",
    )(a.astype(BF16).reshape(nb, 8, r1 * r2, kdim), col_of)
    return out.reshape(nb, r1 * 8 * r2, n)


def _s5_mixer(x, mods, norm_g, ssm, n_ctx):
    a_re, a_im, log_dt, b_re, b_im, c_re, c_im, d_skip, w_glu = ssm
    B, T, _ = x.shape
    nt = T // TM
    nct = n_ctx // TM
    nch = T // S5_T
    nb = N_GROUPS // 8
    sw = 8 * STATE_DIM
    cw = S5_T * 128
    m8, w8, v8, a_pow = _s5_matrices(a_re, a_im, log_dt, b_re, b_im, c_re, c_im)

    h = pl.pallas_call(
        _mod_only_kernel,
        out_shape=_sds((B, T, D), F32),
        grid=(B, nt),
        in_specs=[_x_spec(), _mod_spec(nct), _const_spec((1, D))],
        out_specs=_x_spec(),
        compiler_params=_params(("parallel", "parallel"), 32),
        name="s5_mod",
    )(x, mods, norm_g.reshape(1, D))
    h4 = h.reshape(B * nch, S5_T, D)

    rows = B * nch
    state_shape = _sds((rows, nb * sw), F32)
    s_parts = pl.pallas_call(
        _s5_state_kernel,
        out_shape=(state_shape,) * 4,
        grid=(nb, B),
        in_specs=[pl.BlockSpec((nch, S5_T, 128), lambda g, b: (b, 0, g)),
                  pl.BlockSpec((None, cw, 4 * sw), lambda g, b: (g, 0, 0))],
        out_specs=(pl.BlockSpec((nch, sw), lambda g, b: (b, g)),) * 4,
        compiler_params=_params(("parallel", "parallel"), 56),
        name="s5_state",
    )(h4, w8)

    n_pairs = N_GROUPS // 2
    npb = 8
    view = lambda t: t.reshape(B, nch, n_pairs, 128)
    scan_blk = pl.BlockSpec((None, nch, npb, 128), lambda b, j: (b, 0, j, 0))
    coef_blk = pl.BlockSpec((npb, 128), lambda b, j: (j, 0))
    h_parts = pl.pallas_call(
        functools.partial(_s5_scan_kernel, n_ctx_chunks=n_ctx // S5_T, n_chunks=nch),
        out_shape=(_sds((B, nch, n_pairs, 128), F32),) * 4,
        grid=(B, n_pairs // npb),
        in_specs=[scan_blk] * 4 + [coef_blk] * 4,
        out_specs=(scan_blk,) * 4,
        compiler_params=_params(("parallel", "parallel"), 56),
        name="s5_scan",
    )(*[view(t) for t in s_parts], *a_pow)

    half = S5_T // 2
    y4 = pl.pallas_call(
        _s5_out_kernel,
        out_shape=_sds((rows, S5_T, D), F32),
        grid=(nb, 2, B),
        in_specs=[pl.BlockSpec((nch, S5_T, 128), lambda g, hf, b: (b, 0, g)),
                  pl.BlockSpec((None, cw, half * 128), lambda g, hf, b: (g, 0, hf))]
                 + [pl.BlockSpec((nch, sw), lambda g, hf, b: (b, g))] * 4
                 + [pl.BlockSpec((None, 4 * sw, half * 128), lambda g, hf, b: (g, 0, hf))],
        out_specs=pl.BlockSpec((nch, half, 128), lambda g, hf, b: (b, hf, g)),
        compiler_params=_params(("parallel", "parallel", "parallel"), 56),
        name="s5_out",
    )(h4, m8, *[t.reshape(rows, nb * sw) for t in h_parts], v8)
    y = y4.reshape(B, T, D)

    return pl.pallas_call(
        _glu_kernel,
        out_shape=_sds((B, T, D), F32),
        grid=(B // 2, nt),
        in_specs=[_pair_spec(), _pair_spec(), _pair_mod_spec(nct), _const_spec((1, D)), _const_spec((1, D)),
                  _const_spec((D, 2 * D))],
        out_specs=_pair_spec(),
        compiler_params=_params(("parallel", "parallel"), 48),
        name="s5_glu",
    )(y, x, mods, norm_g.reshape(1, D), d_skip.reshape(1, D), w_glu.astype(BF16))


N_FF_CHUNKS = D_FF // FF_CHUNK


def _swiglu_rows(h, wgu_ref, wd_ref):
    acc = jnp.zeros((h.shape[0], D), F32)
    for c in range(N_FF_CHUNKS):
        lo, hi = c * FF_CHUNK, (c + 1) * FF_CHUNK
        g = jnp.dot(h, wgu_ref[:, lo:hi], preferred_element_type=F32)
        u = jnp.dot(h, wgu_ref[:, D_FF + lo:D_FF + hi], preferred_element_type=F32)
        a = ((g * jax.nn.sigmoid(g)) * u).astype(BF16)
        acc = acc + jnp.dot(a, wd_ref[lo:hi, :], preferred_element_type=F32)
    return acc


def _ffn_kernel(x_ref, mod_ref, g_ref, wgu_ref, wd_ref, o_ref):
    y = _swiglu_rows(_modulate_pair(x_ref, g_ref, mod_ref, 3, 4), wgu_ref, wd_ref)
    for j in range(2):
        o_ref[j] = x_ref[j] + mod_ref[j, 5:6, :] * y[j * TM:(j + 1) * TM, :]


def _dense_ffn(x, mods, norm_g, w_gu, w_down, n_ctx):
    B, T, _ = x.shape
    assert B % 2 == 0
    return pl.pallas_call(
        _ffn_kernel,
        out_shape=_sds((B, T, D), F32),
        grid=(B // 2, T // TM),
        in_specs=[_pair_spec(), _pair_mod_spec(n_ctx // TM),
                  _const_spec((1, D)), _const_spec((D, 2 * D_FF)), _const_spec((D_FF, D))],
        out_specs=_pair_spec(),
        compiler_params=_params(("parallel", "parallel"), 56),
        name="dense_ffn",
    )(x, mods, norm_g.reshape(1, D), w_gu.astype(BF16), w_down.astype(BF16))


def _router_kernel(x_ref, mod_ref, g_ref, whi_ref, wlo_ref, r_out):
    h = _modulate(x_ref[...], g_ref[...], mod_ref[3:4, :], mod_ref[4:5, :])
    h_hi = h.astype(BF16)
    h_lo = (h - h_hi.astype(F32)).astype(BF16)
    logits = (jnp.dot(h_hi, whi_ref[...], preferred_element_type=F32)
              + jnp.dot(h_lo, whi_ref[...], preferred_element_type=F32)
              + jnp.dot(h_hi, wlo_ref[...], preferred_element_type=F32))
    lane = lax.broadcasted_iota(jnp.int32, logits.shape, 1)
    neg = jnp.float32(-jnp.inf)
    logits = jnp.where(lane < N_EXPERTS, logits, neg)
    v1 = jnp.max(logits, axis=-1, keepdims=True)
    i1 = jnp.min(jnp.where(logits == v1, lane, 128), axis=-1, keepdims=True)
    rest = jnp.where(lane == i1, neg, logits)
    v2 = jnp.max(rest, axis=-1, keepdims=True)
    i2 = jnp.min(jnp.where(rest == v2, lane, 128), axis=-1, keepdims=True)
    e = jnp.exp(v2 - v1)
    w1 = 1.0 / (1.0 + e)
    w2 = e / (1.0 + e)
    out_lane = lax.broadcasted_iota(jnp.int32, (TM, 8), 1)
    r = jnp.where(out_lane == 0, i1.astype(F32), 0.0)
    r = jnp.where(out_lane == 1, i2.astype(F32), r)
    r = jnp.where(out_lane == 2, w1, r)
    r = jnp.where(out_lane == 3, w2, r)
    r_out[...] = r


RUN_ALIGN = 8
RUN_BITS = tuple(1 << b for b in range(TM.bit_length() - 1, RUN_ALIGN.bit_length() - 2, -1))
LOCAL_ROWS = 2 * TM + 64


def _run_copies(run_ref, local, remote, sem, *, to_remote, wait):
    for e in range(N_EXPERTS):
        l0, n, g0 = run_ref[0, 0, e], run_ref[0, 1, e], run_ref[0, 2, e]
        done = jnp.int32(0)
        for bit in RUN_BITS:
            take = (n & bit) != 0

            @pl.when(take)
            def _(done=done, bit=bit):
                loc = local.at[pl.ds(pl.multiple_of(l0 + done, RUN_ALIGN), bit)]
                rem = remote.at[pl.ds(pl.multiple_of(g0 + done, RUN_ALIGN), bit)]
                cp = pltpu.make_async_copy(loc, rem, sem) if to_remote else pltpu.make_async_copy(rem, loc, sem)
                if wait:
                    cp.wait()
                else:
                    cp.start()

            done = done + (n & bit)


def _block_onehot(lp, shape, axis):
    return jnp.where(lax.broadcasted_iota(jnp.int32, shape, axis) == lp, 1.0, 0.0).astype(BF16)


def _dispatch_kernel(run_ref, prev_ref, tail_ref, used_ref, x_ref, mod_ref, g_ref, lp_ref, xs_hbm, sbuf, zbuf, sems,
                     *, n_tiles):
    t = pl.program_id(0)
    slot = lax.rem(t, 2)

    @pl.when(t == 0)
    def _():
        zbuf[...] = jnp.zeros_like(zbuf)

        def tile_copy(j):
            return pltpu.make_async_copy(zbuf, xs_hbm.at[pl.ds(pl.multiple_of(j * TME, TME), TME)], sems.at[2])

        def start_tile(j, carry):
            tile_copy(j).start()
            return carry

        def wait_tile(j, carry):
            tile_copy(j).wait()
            return carry

        _run_copies(tail_ref, zbuf, xs_hbm, sems.at[2], to_remote=True, wait=False)
        lax.fori_loop(used_ref[0, 0], n_tiles, start_tile, 0)
        _run_copies(tail_ref, zbuf, xs_hbm, sems.at[2], to_remote=True, wait=True)
        lax.fori_loop(used_ref[0, 0], n_tiles, wait_tile, 0)

    h = _modulate(x_ref[...], g_ref[...], mod_ref[3:4, :], mod_ref[4:5, :]).astype(BF16)
    lp = lp_ref[...]
    place = _block_onehot(lp[0:1, :], (LOCAL_ROWS, TM), 0) + _block_onehot(lp[1:2, :], (LOCAL_ROWS, TM), 0)
    sbuf[slot] = jnp.dot(place, h, preferred_element_type=F32)

    @pl.when(t > 0)
    def _():
        _run_copies(prev_ref, sbuf.at[1 - slot], xs_hbm, sems.at[1 - slot], to_remote=True, wait=True)

    _run_copies(run_ref, sbuf.at[slot], xs_hbm, sems.at[slot], to_remote=True, wait=False)

    @pl.when(t == pl.num_programs(0) - 1)
    def _():
        _run_copies(run_ref, sbuf.at[slot], xs_hbm, sems.at[slot], to_remote=True, wait=True)


def _expert_kernel(meta_ref, xs_ref, wgu_ref, wd_ref, y_ref):
    t = pl.program_id(0)
    n_used = meta_ref[pl.num_programs(0)]

    @pl.when(t < n_used)
    def _():
        y_ref[...] = _swiglu_rows(xs_ref[...].astype(BF16), wgu_ref, wd_ref)

    @pl.when(t >= n_used)
    def _():
        y_ref[...] = jnp.zeros_like(y_ref)


def _combine_kernel(run_ref, next_ref, y_hbm, x_ref, mod_ref, r_ref, lp_ref, fin_ref, o_ref, ybuf, sems, *, final):
    t = pl.program_id(0)
    slot = lax.rem(t, 2)

    def fetch(table, s):
        ybuf[s, 2 * TM:LOCAL_ROWS, :] = jnp.zeros((LOCAL_ROWS - 2 * TM, D), F32)
        _run_copies(table, ybuf.at[s], y_hbm, sems.at[s], to_remote=False, wait=False)

    @pl.when(t == 0)
    def _():
        fetch(run_ref, 0)

    @pl.when(t + 1 < pl.num_programs(0))
    def _():
        fetch(next_ref, 1 - slot)

    _run_copies(run_ref, ybuf.at[slot], y_hbm, sems.at[slot], to_remote=False, wait=True)
    ys = ybuf[slot].astype(BF16)
    lp = lp_ref[...]
    y0 = jnp.dot(_block_onehot(lp[:, 0:1], (TM, LOCAL_ROWS), 1), ys, preferred_element_type=F32)
    y1 = jnp.dot(_block_onehot(lp[:, 1:2], (TM, LOCAL_ROWS), 1), ys, preferred_element_type=F32)
    r = r_ref[...]
    x = x_ref[...] + mod_ref[5:6, :] * (r[:, 2:3] * y0 + r[:, 3:4] * y1)
    if final:
        x = (x * lax.rsqrt(jnp.mean(x * x, axis=-1, keepdims=True) + EPS)) * fin_ref[...]
    o_ref[...] = x


def _moe_ffn(x, mods, norm_g, w_router, w_gu, w_down, n_ctx, final_g=None):
    B, T, _ = x.shape
    nt = T // TM
    nct = n_ctx // TM
    n_tok = B * T
    nblk = B * nt
    wr = jnp.zeros((D, 128), F32).at[:, :N_EXPERTS].set(w_router)
    wr_hi = wr.astype(BF16)
    route = pl.pallas_call(
        _router_kernel,
        out_shape=_sds((B, T, 8), F32),
        grid=(B, nt),
        in_specs=[_x_spec(), _mod_spec(nct), _const_spec((1, D)), _const_spec((D, 128)), _const_spec((D, 128))],
        out_specs=pl.BlockSpec((None, TM, 8), lambda b, i: (b, i, 0)),
        compiler_params=_params(("parallel", "parallel"), 32),
        name="router",
    )(x, mods, norm_g.reshape(1, D), wr_hi, (wr - wr_hi.astype(F32)).astype(BF16))

    ids = jnp.arange(N_EXPERTS, dtype=jnp.int32)
    eid = route.reshape(nblk, TM, 8)[:, :, :2].astype(jnp.int32).reshape(nblk, 2 * TM)
    onehot = (eid[:, :, None] == ids).astype(jnp.int32)
    cnt = jnp.sum(onehot, axis=1)
    cnt = ((cnt + RUN_ALIGN - 1) // RUN_ALIGN) * RUN_ALIGN
    l0 = jnp.cumsum(cnt, axis=1) - cnt
    rank = jnp.sum((jnp.cumsum(onehot, axis=1) - onehot) * onehot, axis=2)
    lp = jnp.sum(onehot * l0[:, None, :], axis=2) + rank
    counts = jnp.sum(cnt, axis=0)
    padded = ((counts + TME - 1) // TME) * TME
    ends = jnp.cumsum(padded)
    g0 = (ends - padded)[None, :] + jnp.cumsum(cnt, axis=0) - cnt
    runs = jnp.stack([l0, cnt, g0], axis=1)
    lp_cols = lp.reshape(nblk, TM, 2)
    lp_rows = lp_cols.transpose(0, 2, 1)
    n_tiles = -(-(2 * n_tok + nblk * N_EXPERTS * (RUN_ALIGN - 1) + N_EXPERTS * (TME - 1)) // TME)
    n_rows = n_tiles * TME
    tile_row = jnp.arange(n_tiles, dtype=jnp.int32)[:, None] * TME
    tile_exp = jnp.minimum(jnp.sum((ends[None, :] <= tile_row).astype(jnp.int32), axis=1), N_EXPERTS - 1)
    meta = jnp.concatenate([tile_exp, (ends[-1:] // TME).astype(jnp.int32)])

    def specs(nto, off):
        blk = lambda t: (t // nto) * nt + t % nto + off
        run = lambda shift, last: pl.BlockSpec(
            (1, 3, N_EXPERTS), lambda t: (blk(jnp.clip(t + shift, 0, last)), 0, 0), memory_space=pltpu.SMEM)
        tok = lambda width: pl.BlockSpec((None, TM, width), lambda t: (t // nto, t % nto + off, 0))
        mod = pl.BlockSpec((None, None, 6, D), lambda t: (t // nto, jnp.where(t % nto + off >= nct, 1, 0), 0, 0))
        return blk, run, tok, mod

    blk, run, tok, mod = specs(nt, 0)
    tails = jnp.stack([jnp.zeros_like(counts), padded - counts, ends - padded + counts])[None]
    xs = pl.pallas_call(
        functools.partial(_dispatch_kernel, n_tiles=n_tiles),
        out_shape=_sds((n_rows, D), F32),
        grid=(nblk,),
        in_specs=[run(0, nblk - 1), run(-1, nblk - 1),
                  pl.BlockSpec((1, 3, N_EXPERTS), lambda t: (0, 0, 0), memory_space=pltpu.SMEM),
                  pl.BlockSpec((1, 1), lambda t: (0, 0), memory_space=pltpu.SMEM),
                  tok(D), mod, pl.BlockSpec((1, D), lambda t: (0, 0)),
                  pl.BlockSpec((None, 2, TM), lambda t: (blk(t), 0, 0))],
        out_specs=pl.BlockSpec(memory_space=pl.ANY),
        scratch_shapes=[pltpu.VMEM((2, LOCAL_ROWS, D), F32), pltpu.VMEM((TME, D), F32),
                        pltpu.SemaphoreType.DMA((3,))],
        compiler_params=_params(("arbitrary",), 40),
        name="moe_dispatch",
    )(runs, runs, tails, meta[n_tiles:].reshape(1, 1), x, mods, norm_g.reshape(1, D), lp_rows)

    y = pl.pallas_call(
        _expert_kernel,
        out_shape=_sds((n_rows, D), F32),
        grid_spec=pltpu.PrefetchScalarGridSpec(
            num_scalar_prefetch=1,
            grid=(n_tiles,),
            in_specs=[pl.BlockSpec((TME, D), lambda t, meta: (t, 0)),
                      pl.BlockSpec((None, D, 2 * D_FF), lambda t, meta: (meta[t], 0, 0)),
                      pl.BlockSpec((None, D_FF, D), lambda t, meta: (meta[t], 0, 0))],
            out_specs=pl.BlockSpec((TME, D), lambda t, meta: (t, 0))),
        compiler_params=_params(("arbitrary",), 56),
        name="experts",
    )(meta, xs, w_gu.astype(BF16), w_down.astype(BF16))

    final = final_g is not None
    off = nct if final else 0
    nto = nt - off
    n_steps = B * nto
    blk, run, tok, mod = specs(nto, off)
    return pl.pallas_call(
        functools.partial(_combine_kernel, final=final),
        out_shape=_sds((B, nto * TM, D), F32),
        grid=(n_steps,),
        in_specs=[run(0, n_steps - 1), run(1, n_steps - 1), pl.BlockSpec(memory_space=pl.ANY), tok(D), mod, tok(8),
                  pl.BlockSpec((None, TM, 2), lambda t: (blk(t), 0, 0)),
                  pl.BlockSpec((1, D), lambda t: (0, 0))],
        out_specs=pl.BlockSpec((None, TM, D), lambda t: (t // nto, t % nto, 0)),
        scratch_shapes=[pltpu.VMEM((2, LOCAL_ROWS, D), F32), pltpu.SemaphoreType.DMA((2,))],
        compiler_params=_params(("arbitrary",), 40),
        name="moe_combine",
    )(runs, runs, y, x, mods, route, lp_cols, (final_g if final else norm_g).reshape(1, D))


def kernel(x, c, ctx, c_ctx, l0_ada_w, l0_ada_b, l0_norm_mix, l0_norm_ffn, l0_conv_w_in, l0_conv_w, l0_conv_w_out, l0_ffn_w_gu, l0_ffn_w_down, l1_ada_w, l1_ada_b, l1_norm_mix, l1_norm_ffn, l1_attn_w_qkv, l1_attn_lam, l1_attn_subln, l1_attn_w_o, l1_moe_router, l1_moe_w_gu, l1_moe_w_down, l2_ada_w, l2_ada_b, l2_norm_mix, l2_norm_ffn, l2_ssm_a_re, l2_ssm_a_im, l2_ssm_log_dt, l2_ssm_b_re, l2_ssm_b_im, l2_ssm_c_re, l2_ssm_c_im, l2_ssm_d, l2_ssm_w_glu, l2_ffn_w_gu, l2_ffn_w_down, l3_ada_w, l3_ada_b, l3_norm_mix, l3_norm_ffn, l3_conv_w_in, l3_conv_w, l3_conv_w_out, l3_moe_router, l3_moe_w_gu, l3_moe_w_down, final_norm):
    B, n_lat, _ = x.shape
    n_ctx = ctx.shape[1]
    T = n_ctx + n_lat
    assert n_ctx % TM == 0 and n_lat % TM == 0 and n_lat % GRID_W == 0 and B < 8 and B % 2 == 0
    xs = jnp.concatenate([ctx, x], axis=1)
    cond8 = jnp.concatenate([c, c_ctx[None], jnp.zeros((8 - B - 1, D), F32)], axis=0)

    def mods_of(w, b):
        m = _ada(cond8, w, b).reshape(8, 6, D)
        return jnp.stack([jnp.broadcast_to(m[B], (B, 6, D)), m[:B]], axis=1)

    mods = mods_of(l0_ada_w, l0_ada_b)
    xs = _conv_mixer(xs, mods, l0_norm_mix, l0_conv_w_in, l0_conv_w, l0_conv_w_out, n_ctx)
    xs = _dense_ffn(xs, mods, l0_norm_ffn, l0_ffn_w_gu, l0_ffn_w_down, n_ctx)

    mods = mods_of(l1_ada_w, l1_ada_b)
    xs = _attn_mixer(xs, mods, l1_norm_mix, l1_attn_w_qkv, l1_attn_lam, l1_attn_subln, l1_attn_w_o, n_ctx,
                     0.8 - 0.6 * math.exp(-0.3 * 1))
    xs = _moe_ffn(xs, mods, l1_norm_ffn, l1_moe_router, l1_moe_w_gu, l1_moe_w_down, n_ctx)

    mods = mods_of(l2_ada_w, l2_ada_b)
    xs = _s5_mixer(xs, mods, l2_norm_mix,
                   (l2_ssm_a_re, l2_ssm_a_im, l2_ssm_log_dt, l2_ssm_b_re, l2_ssm_b_im, l2_ssm_c_re, l2_ssm_c_im,
                    l2_ssm_d, l2_ssm_w_glu), n_ctx)
    xs = _dense_ffn(xs, mods, l2_norm_ffn, l2_ffn_w_gu, l2_ffn_w_down, n_ctx)

    mods = mods_of(l3_ada_w, l3_ada_b)
    xs = _conv_mixer(xs, mods, l3_norm_mix, l3_conv_w_in, l3_conv_w, l3_conv_w_out, n_ctx)
    return _moe_ffn(xs, mods, l3_norm_ffn, l3_moe_router, l3_moe_w_gu, l3_moe_w_down, n_ctx, final_g=final_norm)
```
